```python
import jax, jax.numpy as jnp
from jax import lax
import numpy as np

D_MODEL = 1024
BATCH = 4
SEQ = 4096
DEPTH = 4

N_META = 16
D_FF = 4 * D_MODEL
D_CONV = D_MODEL // 2
CONV_WIDTH = 31
N_POOL_GROUPS = 4
POOL_WINDOWS = (2, 4, 8, 16)
D_POOL = D_MODEL // 2
POOL_GROUP_DIM = D_POOL // N_POOL_GROUPS
D_EVEN_IN = 2 * D_CONV + D_POOL
HGRN_HEAD_DIM = 128
HGRN_HEADS = D_MODEL // HGRN_HEAD_DIM
D_HGRN = HGRN_HEADS * HGRN_HEAD_DIM
CHUNK = 64
N_EVEN = (DEPTH + 1) // 2
N_ODD = DEPTH // 2
EPS = 1e-6

kernel_name = 'hybrid_conv_pool_hgrn2_trunk'


def _rmsnorm(x, g):
    xf = x.astype(jnp.float32)
    y = xf * lax.rsqrt(jnp.mean(xf * xf, axis=-1, keepdims=True) + EPS)
    return (y * g.astype(jnp.float32)).astype(x.dtype)


def _layernorm(x, g, b):
    xf = x.astype(jnp.float32)
    mu = jnp.mean(xf, axis=-1, keepdims=True)
    xc = xf - mu
    y = xc * lax.rsqrt(jnp.mean(xc * xc, axis=-1, keepdims=True) + EPS)
    return (y * g.astype(jnp.float32) + b.astype(jnp.float32)).astype(x.dtype)


def _conv_mixer(val, gate, conv_w, conv_b, ln_g, ln_b):
    a = val * jax.nn.sigmoid(gate)
    y = lax.conv_general_dilated(
        a, conv_w[:, None, :].astype(a.dtype), window_strides=(1,),
        padding=[(CONV_WIDTH - 1, 0)], dimension_numbers=('NWC', 'WIO', 'NWC'),
        feature_group_count=D_CONV) + conv_b
    return jax.nn.silu(_layernorm(y, ln_g, ln_b))


def _causal_window_mean(x, w):
    L = x.shape[1]
    cs = jnp.cumsum(x.astype(jnp.float32), axis=1)
    cs0 = jnp.pad(cs, ((0, 0), (1, 0), (0, 0)))
    lower = jnp.pad(cs0[:, :L + 1 - w], ((0, 0), (w - 1, 0), (0, 0)))
    count = jnp.minimum(jnp.arange(1, L + 1, dtype=jnp.float32), float(w))
    return ((cs - lower) / count[None, :, None]).astype(x.dtype)


def _pool_mixer(u, pool_w, pool_b, pool_scale):
    Bn, L, _ = u.shape
    ug = u.reshape(Bn, L, N_POOL_GROUPS, POOL_GROUP_DIM)
    pooled = jnp.stack([_causal_window_mean(ug[:, :, gi], w) for gi, w in enumerate(POOL_WINDOWS)], axis=2)
    y = jnp.einsum('blgc,gcd->blgd', pooled - ug, pool_w) + pool_b
    return y.reshape(Bn, L, D_POOL) * pool_scale


def _hgrn2_chunk_scan(q, k, v, logf):
    C = q.shape[3]
    causal = jnp.tril(jnp.ones((C, C), dtype=bool))

    def step(S, inp):
        qc, kc, vc, lfc = inp
        b = jnp.cumsum(lfc, axis=2)
        diff = b[:, :, :, None, :] - b[:, :, None, :, :]
        decay = jnp.exp(jnp.where(causal[:, :, None], diff, -jnp.inf))
        scores = jnp.einsum('bhtk,bhsk,bhtsk->bhts', qc, kc, decay)
        o = (jnp.einsum('bhts,bhsv->bhtv', scores, vc)
             + jnp.einsum('bhtk,bhkv->bhtv', qc * jnp.exp(b), S))
        b_last = b[:, :, -1:, :]
        S = (jnp.exp(b_last[:, :, 0, :])[..., None] * S
             + jnp.einsum('bhsk,bhsv->bhkv', kc * jnp.exp(b_last - b), vc))
        return S, o

    S0 = jnp.zeros((q.shape[1], q.shape[2], q.shape[4], v.shape[4]), jnp.float32)
    _, o = lax.scan(step, S0, (q, k, v, logf))
    return o


def _hgrn2_mixer(u, lb, gnorm_g):
    Bn, L, _ = u.shape
    q, f, i, g = jnp.split(u, 4, axis=-1)
    q = jax.nn.silu(q.astype(jnp.float32))
    forget = lb + (1.0 - lb) * jax.nn.sigmoid(f.astype(jnp.float32))
    k = 1.0 - forget
    logf = jnp.log(forget)
    v = i.astype(jnp.float32)
    pad = CHUNK - N_META
    Lp = L + pad
    n_chunks = Lp // CHUNK

    def to_chunks(t):
        t = jnp.pad(t, ((0, 0), (pad, 0), (0, 0)))
        t = t.reshape(Bn, n_chunks, CHUNK, HGRN_HEADS, HGRN_HEAD_DIM)
        return t.transpose(1, 0, 3, 2, 4)

    o = _hgrn2_chunk_scan(to_chunks(q), to_chunks(k), to_chunks(v), to_chunks(logf))
    o = o.transpose(1, 0, 3, 2, 4).reshape(Bn, Lp, HGRN_HEADS, HGRN_HEAD_DIM)[:, pad:]
    gh = g.reshape(Bn, L, HGRN_HEADS, HGRN_HEAD_DIM).astype(jnp.float32)
    o = _rmsnorm(o, gnorm_g) * jax.nn.silu(gh)
    return o.reshape(Bn, L, D_HGRN).astype(u.dtype)


def setup_inputs(seed: int = 0) -> dict:
    key = jax.random.key(seed)
    ks = jax.random.split(key, 24)
    f32 = jnp.float32
    nrm = lambda k, shape, s: jax.random.normal(k, shape, f32) * s
    return {
        'x': nrm(ks[0], (BATCH, SEQ, D_MODEL), 1.0),
        'meta_tokens': nrm(ks[1], (N_META, D_MODEL), 1.0),
        'mix_norm_g': 1.0 + nrm(ks[2], (DEPTH, D_MODEL), 0.02),
        'mlp_norm_g': 1.0 + nrm(ks[3], (DEPTH, D_MODEL), 0.02),
        'final_norm_g': 1.0 + nrm(ks[4], (D_MODEL,), 0.02),
        'ev_w_in': nrm(ks[5], (N_EVEN, D_MODEL, D_EVEN_IN), D_MODEL ** -0.5),
        'ev_conv_w': nrm(ks[6], (N_EVEN, CONV_WIDTH, D_CONV), CONV_WIDTH ** -0.5),
        'ev_conv_b': nrm(ks[7], (N_EVEN, D_CONV), 0.01),
        'ev_ln_g': 1.0 + nrm(ks[8], (N_EVEN, D_CONV), 0.02),
        'ev_ln_b': nrm(ks[9], (N_EVEN, D_CONV), 0.01),
        'ev_pool_w': nrm(ks[10], (N_EVEN, N_POOL_GROUPS, POOL_GROUP_DIM, POOL_GROUP_DIM), POOL_GROUP_DIM ** -0.5),
        'ev_pool_b': nrm(ks[11], (N_EVEN, N_POOL_GROUPS, POOL_GROUP_DIM), 0.01),
        'ev_pool_scale': 1.0 + nrm(ks[12], (N_EVEN, D_POOL), 0.02),
        'ev_w_out': nrm(ks[13], (N_EVEN, D_CONV + D_POOL, D_MODEL), (D_CONV + D_POOL) ** -0.5),
        'od_w_in': nrm(ks[14], (N_ODD, D_MODEL, 4 * D_HGRN), D_MODEL ** -0.5),
        'od_gnorm_g': 1.0 + nrm(ks[15], (N_ODD, HGRN_HEAD_DIM), 0.02),
        'od_w_out': nrm(ks[16], (N_ODD, D_HGRN, D_MODEL), D_HGRN ** -0.5),
        'lb_param': nrm(ks[17], (DEPTH, D_HGRN), 1.0),
        'mlp_w1': nrm(ks[18], (DEPTH, D_MODEL, D_FF), D_MODEL ** -0.5),
        'mlp_w2': nrm(ks[19], (DEPTH, D_FF, D_MODEL), D_FF ** -0.5),
    }


def reference(x, meta_tokens, mix_norm_g, mlp_norm_g, final_norm_g,
              ev_w_in, ev_conv_w, ev_conv_b, ev_ln_g, ev_ln_b,
              ev_pool_w, ev_pool_b, ev_pool_scale, ev_w_out,
              od_w_in, od_gnorm_g, od_w_out, lb_param, mlp_w1, mlp_w2):
    Bn = x.shape[0]
    meta = jnp.broadcast_to(meta_tokens[None].astype(x.dtype), (Bn, N_META, D_MODEL))
    h = jnp.concatenate([meta, x], axis=1)
    lb_all = jnp.cumsum(jax.nn.softmax(lb_param.astype(jnp.float32), axis=0), axis=0)
    lb_all = lb_all - lb_all[0]
    for layer in range(DEPTH):
        j = layer // 2
        n = _rmsnorm(h, mix_norm_g[layer])
        if layer % 2 == 0:
            u = n @ ev_w_in[j]
            val, gate, pin = jnp.split(u, [D_CONV, 2 * D_CONV], axis=-1)
            ya = _conv_mixer(val, gate, ev_conv_w[j], ev_conv_b[j], ev_ln_g[j], ev_ln_b[j])
            yb = _pool_mixer(pin, ev_pool_w[j], ev_pool_b[j], ev_pool_scale[j])
            h = h + jnp.concatenate([ya, yb], axis=-1) @ ev_w_out[j]
        else:
            u = n @ od_w_in[j]
            y = _hgrn2_mixer(u, lb_all[layer], od_gnorm_g[j])
            h = h + y @ od_w_out[j]
        n = _rmsnorm(h, mlp_norm_g[layer])
        h = h + jnp.square(jax.nn.relu(n @ mlp_w1[layer])) @ mlp_w2[layer]
    return _rmsnorm(h, final_norm_g)[:, N_META:]
```

```python
import functools

import jax
import jax.numpy as jnp
from jax import lax
from jax.experimental import pallas as pl
from jax.experimental.pallas import tpu as pltpu

F32 = jnp.float32
BF16 = jnp.bfloat16

N_META = 16
CONV_WIDTH = 31
POOL_WINDOWS = (2, 4, 8, 16)
HGRN_HEAD_DIM = 128
EPS = 1e-6
LANES = 128

CHUNK = 64
SEQ_TILE = 832
ROW_CHUNK = 64
CONV_HALO = 32
POOL_HALO = 16
FF_CHUNK = 1024
VMEM_LIMIT = 56 * 1024 * 1024


def _sigmoid(x):
    return 1.0 / (1.0 + jnp.exp(-x))


def _rmsnorm(x, g):
    return x * lax.rsqrt(jnp.mean(x * x, axis=-1, keepdims=True) + EPS) * g


def _dot(a, b):
    return jnp.dot(a, b, preferred_element_type=F32)


def _dot_nt(a, b):
    return lax.dot_general(a, b, (((1,), (1,)), ((), ())), preferred_element_type=F32)


def _dot_tn(a, b):
    return lax.dot_general(a, b, (((0,), (0,)), ((), ())), preferred_element_type=F32)


def _zero_pad_rows(y, first_row, l_real):
    row = first_row + lax.broadcasted_iota(jnp.int32, (y.shape[0], 1), 0)
    return jnp.where(row < l_real, y, 0.0)


def _resident(shape):
    nd = len(shape)
    return pl.BlockSpec(shape, lambda *_: (0,) * nd, pipeline_mode=pl.Buffered(1))


def _mlp_kernel(h_ref, g_ref, fg_ref, w1_ref, w2_ref, o_ref, *, final):
    x = h_ref[...]
    n = _rmsnorm(x, g_ref[...]).astype(BF16)
    acc = x
    for c in range(w1_ref.shape[1] // FF_CHUNK):
        cols = slice(c * FF_CHUNK, (c + 1) * FF_CHUNK)
        hid = _dot(n, w1_ref[:, cols])
        hid = jnp.square(jnp.maximum(hid, 0.0)).astype(BF16)
        acc = acc + _dot(hid, w2_ref[cols, :])
    if final:
        acc = _rmsnorm(acc, fg_ref[...])
    o_ref[...] = acc


def _mlp(h2d, g, fg, w1, w2, *, final):
    rows, d = h2d.shape
    tile = pl.BlockSpec((SEQ_TILE, d), lambda i: (i, 0))
    return pl.pallas_call(
        functools.partial(_mlp_kernel, final=final),
        grid=(rows // SEQ_TILE,),
        in_specs=[tile, _resident(g.shape), _resident(fg.shape), _resident(w1.shape), _resident(w2.shape)],
        out_specs=tile,
        out_shape=jax.ShapeDtypeStruct(h2d.shape, F32),
        compiler_params=pltpu.CompilerParams(
            dimension_semantics=("arbitrary",), vmem_limit_bytes=VMEM_LIMIT),
        name="mlp_final" if final else "mlp",
    )(h2d, g, fg, w1, w2)


def _even_kernel(h_ref, g_ref, win_ref, cw_ref, cb_ref, lng_ref, lnb_ref, pw_ref, pb_ref, ps_ref,
                 wout_ref, o_ref, a_ext, p_ext, d_ref, y_ref, *, l_real):
    tl = h_ref.shape[0]
    dc = cw_ref.shape[1]
    gd = pw_ref.shape[1]
    l = pl.program_id(1)

    x = h_ref[...]
    n = _rmsnorm(x, g_ref[...]).astype(BF16)
    u = _dot(n, win_ref[...])

    @pl.when(l == 0)
    def _():
        a_ext[0:CONV_HALO, :] = jnp.zeros((CONV_HALO, dc), F32)
        p_ext[0:POOL_HALO, :] = jnp.zeros((POOL_HALO, p_ext.shape[1]), F32)

    @pl.when(l > 0)
    def _():
        a_ext[0:CONV_HALO, :] = a_ext[tl:tl + CONV_HALO, :]
        p_ext[0:POOL_HALO, :] = p_ext[tl:tl + POOL_HALO, :]

    a_ext[CONV_HALO:CONV_HALO + tl, :] = u[:, 0:dc] * _sigmoid(u[:, dc:2 * dc])
    p_ext[POOL_HALO:POOL_HALO + tl, :] = u[:, 2 * dc:]

    def chunk(r, carry):
        r0 = pl.multiple_of(r * ROW_CHUNK, ROW_CHUNK)
        parts = []
        for lt in range(dc // LANES):
            lanes = slice(lt * LANES, (lt + 1) * LANES)
            win = a_ext[pl.ds(r0, ROW_CHUNK + CONV_HALO), lanes]
            part = jnp.broadcast_to(cb_ref[:, lanes], (ROW_CHUNK, LANES))
            for j in range(CONV_WIDTH):
                off = CONV_HALO - (CONV_WIDTH - 1) + j
                part = part + cw_ref[j:j + 1, lanes] * win[off:off + ROW_CHUNK, :]
            parts.append(part)
        acc = jnp.concatenate(parts, axis=1)
        mu = jnp.mean(acc, axis=-1, keepdims=True)
        xc = acc - mu
        yn = xc * lax.rsqrt(jnp.mean(xc * xc, axis=-1, keepdims=True) + EPS) * lng_ref[...] + lnb_ref[...]
        y_ref[pl.ds(r0, ROW_CHUNK), 0:dc] = (yn * _sigmoid(yn)).astype(BF16)

        pos = l * tl + r0 + lax.broadcasted_iota(jnp.int32, (ROW_CHUNK, 1), 0)
        for gi, w in enumerate(POOL_WINDOWS):
            lanes = slice(gi * gd, (gi + 1) * gd)
            win = p_ext[pl.ds(r0, ROW_CHUNK + POOL_HALO), lanes]
            cur = win[POOL_HALO:, :]
            s = cur
            for j in range(1, w):
                s = s + win[POOL_HALO - j:POOL_HALO - j + ROW_CHUNK, :]
            cnt = jnp.minimum(pos + 1, w).astype(F32)
            d_ref[pl.ds(r0, ROW_CHUNK), lanes] = (s / cnt - cur).astype(BF16)
        return carry

    lax.fori_loop(0, tl // ROW_CHUNK, chunk, 0)

    for gi in range(len(POOL_WINDOWS)):
        lanes = slice(gi * gd, (gi + 1) * gd)
        yb = (_dot(d_ref[:, lanes], pw_ref[gi]) + pb_ref[:, lanes]) * ps_ref[:, lanes]
        y_ref[:, dc + gi * gd:dc + (gi + 1) * gd] = yb.astype(BF16)

    o_ref[...] = _zero_pad_rows(x + _dot(y_ref[...], wout_ref[...]), l * tl, l_real)


def _even_mixer(h, g, w_in, conv_w, conv_b, ln_g, ln_b, pool_w, pool_b, pool_scale, w_out, *, l_real):
    b, lp, d = h.shape
    dc = conv_w.shape[1]
    dp = pool_scale.shape[1]
    tile = pl.BlockSpec((None, SEQ_TILE, d), lambda i, j: (i, j, 0))
    params = (g, w_in, conv_w, conv_b, ln_g, ln_b, pool_w, pool_b, pool_scale, w_out)
    return pl.pallas_call(
        functools.partial(_even_kernel, l_real=l_real),
        grid=(b, lp // SEQ_TILE),
        in_specs=[tile] + [_resident(p.shape) for p in params],
        out_specs=tile,
        out_shape=jax.ShapeDtypeStruct(h.shape, F32),
        scratch_shapes=[
            pltpu.VMEM((CONV_HALO + SEQ_TILE, dc), F32),
            pltpu.VMEM((POOL_HALO + SEQ_TILE, dp), F32),
            pltpu.VMEM((SEQ_TILE, dp), BF16),
            pltpu.VMEM((SEQ_TILE, dc + dp), BF16),
        ],
        compiler_params=pltpu.CompilerParams(
            dimension_semantics=("arbitrary", "arbitrary"), vmem_limit_bytes=VMEM_LIMIT),
        name="even_mixer",
    )(h, *params)


def _chunk_cumsum(x):
    row = lax.broadcasted_iota(jnp.int32, (x.shape[0], 1), 0)
    d = 1
    while d < x.shape[0]:
        x = x + jnp.where(row >= d, pltpu.roll(x, d, axis=0), 0.0)
        d *= 2
    return x


def _odd_kernel(h_ref, g_ref, win_ref, lb_ref, gng_ref, wout_ref, o_ref,
                q_ref, f_ref, v_ref, gate_ref, qt_ref, kt_ref, kd_ref, y_ref, st_ref, *, l_real):
    tl, d = h_ref.shape
    hd = HGRN_HEAD_DIM
    heads = d // hd
    l = pl.program_id(1)

    @pl.when(l == 0)
    def _():
        st_ref[...] = jnp.zeros(st_ref.shape, F32)

    x = h_ref[...]
    n = _rmsnorm(x, g_ref[...]).astype(BF16)
    q = _dot(n, win_ref[:, 0:d])
    q_ref[...] = q * _sigmoid(q)
    f_ref[...] = _dot(n, win_ref[:, d:2 * d])
    v_ref[...] = _dot(n, win_ref[:, 2 * d:3 * d]).astype(BF16)
    gt = _dot(n, win_ref[:, 3 * d:4 * d])
    gate_ref[...] = (gt * _sigmoid(gt)).astype(BF16)

    causal = (lax.broadcasted_iota(jnp.int32, (CHUNK, CHUNK), 0)
              >= lax.broadcasted_iota(jnp.int32, (CHUNK, CHUNK), 1))

    def chunk(c, carry):
        rows = pl.ds(pl.multiple_of(c * CHUNK, CHUNK), CHUNK)
        lb = lb_ref[...]
        forget = lb + (1.0 - lb) * _sigmoid(f_ref[rows, :])
        k = 1.0 - forget
        b = _chunk_cumsum(jnp.log(forget))
        b_last = b[CHUNK - 1:CHUNK, :]
        qt_ref[rows, :] = (q_ref[rows, :] * jnp.exp(b)).astype(BF16)
        kt_ref[rows, :] = (k * jnp.exp(-b)).astype(BF16)
        kd_ref[rows, :] = (k * jnp.exp(b_last - b)).astype(BF16)
        decay = jnp.exp(b_last)

        for hh in range(heads):
            lanes = slice(hh * hd, (hh + 1) * hd)
            qt = qt_ref[rows, lanes]
            vv = v_ref[rows, lanes]
            s = jnp.where(causal, _dot_nt(qt, kt_ref[rows, lanes]), 0.0).astype(BF16)
            st = st_ref[hh]
            o = _dot(s, vv) + _dot_nt(qt, st.astype(BF16))
            st_ref[hh] = st * decay[:, lanes] + _dot_tn(vv, kd_ref[rows, lanes])
            on = o * lax.rsqrt(jnp.mean(o * o, axis=-1, keepdims=True) + EPS) * gng_ref[...]
            y_ref[rows, lanes] = (on * gate_ref[rows, lanes].astype(F32)).astype(BF16)
        return carry

    lax.fori_loop(0, tl // CHUNK, chunk, 0)

    o_ref[...] = _zero_pad_rows(x + _dot(y_ref[...], wout_ref[...]), l * tl, l_real)


def _odd_mixer(h, g, w_in, lb, gn_g, w_out, *, l_real):
    b, lp, d = h.shape
    heads = d // HGRN_HEAD_DIM
    tile = pl.BlockSpec((None, SEQ_TILE, d), lambda i, j: (i, j, 0))
    params = (g, w_in, lb, gn_g, w_out)
    big = lambda dt: pltpu.VMEM((SEQ_TILE, d), dt)
    return pl.pallas_call(
        functools.partial(_odd_kernel, l_real=l_real),
        grid=(b, lp // SEQ_TILE),
        in_specs=[tile] + [_resident(p.shape) for p in params],
        out_specs=tile,
        out_shape=jax.ShapeDtypeStruct(h.shape, F32),
        scratch_shapes=[big(F32), big(F32), big(BF16), big(BF16), big(BF16), big(BF16), big(BF16), big(BF16),
                        pltpu.VMEM((heads, HGRN_HEAD_DIM, HGRN_HEAD_DIM), F32)],
        compiler_params=pltpu.CompilerParams(
            dimension_semantics=("arbitrary", "arbitrary"), vmem_limit_bytes=VMEM_LIMIT),
        name="odd_mixer",
    )(h, *params)


def kernel(x, meta_tokens, mix_norm_g, mlp_norm_g, final_norm_g, ev_w_in, ev_conv_w, ev_conv_b, ev_ln_g,
           ev_ln_b, ev_pool_w, ev_pool_b, ev_pool_scale, ev_w_out, od_w_in, od_gnorm_g, od_w_out, lb_param,
           mlp_w1, mlp_w2):
    bn, seq, d = x.shape
    depth = mix_norm_g.shape[0]
    l_real = N_META + seq
    l_pad = -(-l_real // SEQ_TILE) * SEQ_TILE

    meta = jnp.broadcast_to(meta_tokens[None].astype(x.dtype), (bn, N_META, d))
    h = jnp.concatenate([meta, x, jnp.zeros((bn, l_pad - l_real, d), x.dtype)], axis=1)

    lb_all = jnp.cumsum(jax.nn.softmax(lb_param.astype(F32), axis=0), axis=0)
    lb_all = lb_all - lb_all[0]

    row = lambda v: v.reshape(1, -1).astype(F32)
    for layer in range(depth):
        j = layer // 2
        if layer % 2 == 0:
            h = _even_mixer(h, row(mix_norm_g[layer]), ev_w_in[j].astype(BF16), ev_conv_w[j], row(ev_conv_b[j]),
                            row(ev_ln_g[j]), row(ev_ln_b[j]), ev_pool_w[j].astype(BF16), row(ev_pool_b[j]),
                            row(ev_pool_scale[j]), ev_w_out[j].astype(BF16), l_real=l_real)
        else:
            h = _odd_mixer(h, row(mix_norm_g[layer]), od_w_in[j].astype(BF16), row(lb_all[layer]),
                           row(od_gnorm_g[j]), od_w_out[j].astype(BF16), l_real=l_real)
        h = _mlp(h.reshape(bn * l_pad, d), row(mlp_norm_g[layer]), row(final_norm_g),
                 mlp_w1[layer].astype(BF16), mlp_w2[layer].astype(BF16),
                 final=(layer == depth - 1)).reshape(bn, l_pad, d)
    return h[:, N_META:l_real]
```

```python
import functools

import jax
import jax.numpy as jnp
from jax import lax
from jax.experimental import pallas as pl
from jax.experimental.pallas import tpu as pltpu

F32 = jnp.float32
BF16 = jnp.bfloat16

N_META = 16
CONV_WIDTH = 31
POOL_WINDOWS = (2, 4, 8, 16)
HGRN_HEAD_DIM = 128
EPS = 1e-6
LANES = 128
SUBLANES = 8

CHUNK = 64
MAX_EXPONENT = 80.0
SEQ_TILE = 832
ROW_CHUNK = 64
CONV_HALO = 32
POOL_HALO = 16
FF_CHUNK = 1024
VMEM_LIMIT = 56 * 1024 * 1024

assert all(w & (w - 1) == 0 and w - 1 <= POOL_HALO for w in POOL_WINDOWS)
assert CONV_WIDTH - 1 <= CONV_HALO


def _sigmoid(x):
    return 1.0 / (1.0 + jnp.exp(-x))


def _rmsnorm(x, g):
    return x * lax.rsqrt(jnp.mean(x * x, axis=-1, keepdims=True) + EPS) * g


def _dot(a, b):
    return jnp.dot(a, b, preferred_element_type=F32)


def _dot_nt(a, b):
    return lax.dot_general(a, b, (((1,), (1,)), ((), ())), preferred_element_type=F32)


def _dot_tn(a, b):
    return lax.dot_general(a, b, (((0,), (0,)), ((), ())), preferred_element_type=F32)


def _zero_pad_rows(y, first_row, l_real):
    row = first_row + lax.broadcasted_iota(jnp.int32, (y.shape[0], 1), 0)
    return jnp.where(row < l_real, y, 0.0)


def _resident(shape):
    nd = len(shape)
    return pl.BlockSpec(shape, lambda *_: (0,) * nd, pipeline_mode=pl.Buffered(1))


def _mlp_kernel(h_ref, g_ref, fg_ref, w1_ref, w2_ref, o_ref, *, final):
    x = h_ref[...]
    n = _rmsnorm(x, g_ref[...]).astype(BF16)
    acc = x
    for c in range(w1_ref.shape[1] // FF_CHUNK):
        cols = slice(c * FF_CHUNK, (c + 1) * FF_CHUNK)
        hid = _dot(n, w1_ref[:, cols])
        hid = jnp.square(jnp.maximum(hid, 0.0)).astype(BF16)
        acc = acc + _dot(hid, w2_ref[cols, :])
    if final:
        acc = _rmsnorm(acc, fg_ref[...])
    o_ref[...] = acc


def _mlp(h2d, g, fg, w1, w2, *, final):
    rows, d = h2d.shape
    tile = pl.BlockSpec((SEQ_TILE, d), lambda i: (i, 0))
    return pl.pallas_call(
        functools.partial(_mlp_kernel, final=final),
        grid=(rows // SEQ_TILE,),
        in_specs=[tile, _resident(g.shape), _resident(fg.shape), _resident(w1.shape), _resident(w2.shape)],
        out_specs=tile,
        out_shape=jax.ShapeDtypeStruct(h2d.shape, F32),
        compiler_params=pltpu.CompilerParams(
            dimension_semantics=("arbitrary",), vmem_limit_bytes=VMEM_LIMIT),
        name="mlp_final" if final else "mlp",
    )(h2d, g, fg, w1, w2)


def _even_kernel(h_ref, g_ref, win_ref, cw_ref, cb_ref, lng_ref, lnb_ref, pw_ref, pb_ref, ps_ref,
                 wout_ref, o_ref, a_ext, p_ext, d_ref, y_ref, *, l_real):
    tl = h_ref.shape[0]
    dc = cw_ref.shape[1]
    gd = pw_ref.shape[1]
    l = pl.program_id(1)

    x = h_ref[...]
    n = _rmsnorm(x, g_ref[...]).astype(BF16)
    u = _dot(n, win_ref[...])

    @pl.when(l == 0)
    def _():
        a_ext[0:CONV_HALO, :] = jnp.zeros((CONV_HALO, dc), F32)
        p_ext[0:POOL_HALO, :] = jnp.zeros((POOL_HALO, p_ext.shape[1]), F32)

    @pl.when(l > 0)
    def _():
        a_ext[0:CONV_HALO, :] = a_ext[tl:tl + CONV_HALO, :]
        p_ext[0:POOL_HALO, :] = p_ext[tl:tl + POOL_HALO, :]

    a_ext[CONV_HALO:CONV_HALO + tl, :] = u[:, 0:dc] * _sigmoid(u[:, dc:2 * dc])
    p_ext[POOL_HALO:POOL_HALO + tl, :] = u[:, 2 * dc:]

    def chunk(r, carry):
        r0 = pl.multiple_of(r * ROW_CHUNK, ROW_CHUNK)
        parts = []
        for lt in range(dc // LANES):
            lanes = slice(lt * LANES, (lt + 1) * LANES)
            win = a_ext[pl.ds(r0, ROW_CHUNK + CONV_HALO), lanes]
            part = jnp.broadcast_to(cb_ref[:, lanes], (ROW_CHUNK, LANES))
            for rem in range(SUBLANES):
                taps = [j for j in range(CONV_WIDTH) if (CONV_HALO - (CONV_WIDTH - 1) + j) % SUBLANES == rem]
                shifted = win if rem == 0 else pltpu.roll(win, win.shape[0] - rem, axis=0)
                for j in taps:
                    base = CONV_HALO - (CONV_WIDTH - 1) + j - rem
                    part = part + cw_ref[j:j + 1, lanes] * shifted[base:base + ROW_CHUNK, :]
            parts.append(part)
        acc = jnp.concatenate(parts, axis=1)
        mu = jnp.mean(acc, axis=-1, keepdims=True)
        xc = acc - mu
        yn = xc * lax.rsqrt(jnp.mean(xc * xc, axis=-1, keepdims=True) + EPS) * lng_ref[...] + lnb_ref[...]
        y_ref[pl.ds(r0, ROW_CHUNK), 0:dc] = (yn * _sigmoid(yn)).astype(BF16)

        pos = l * tl + r0 + lax.broadcasted_iota(jnp.int32, (ROW_CHUNK, 1), 0)
        for gi, w in enumerate(POOL_WINDOWS):
            lanes = slice(gi * gd, (gi + 1) * gd)
            win = p_ext[pl.ds(r0, ROW_CHUNK + POOL_HALO), lanes]
            cur = win[POOL_HALO:, :]
            s = win
            k = 1
            while k < w:
                s = s + pltpu.roll(s, k, axis=0)
                k *= 2
            s = s[POOL_HALO:, :]
            cnt = jnp.minimum(pos + 1, w).astype(F32)
            d_ref[pl.ds(r0, ROW_CHUNK), lanes] = (s / cnt - cur).astype(BF16)
        return carry

    lax.fori_loop(0, tl // ROW_CHUNK, chunk, 0)

    for gi in range(len(POOL_WINDOWS)):
        lanes = slice(gi * gd, (gi + 1) * gd)
        yb = (_dot(d_ref[:, lanes], pw_ref[gi]) + pb_ref[:, lanes]) * ps_ref[:, lanes]
        y_ref[:, dc + gi * gd:dc + (gi + 1) * gd] = yb.astype(BF16)

    o_ref[...] = _zero_pad_rows(x + _dot(y_ref[...], wout_ref[...]), l * tl, l_real)


def _even_mixer(h, g, w_in, conv_w, conv_b, ln_g, ln_b, pool_w, pool_b, pool_scale, w_out, *, l_real):
    b, lp, d = h.shape
    dc = conv_w.shape[1]
    dp = pool_scale.shape[1]
    tile = pl.BlockSpec((None, SEQ_TILE, d), lambda i, j: (i, j, 0))
    params = (g, w_in, conv_w, conv_b, ln_g, ln_b, pool_w, pool_b, pool_scale, w_out)
    return pl.pallas_call(
        functools.partial(_even_kernel, l_real=l_real),
        grid=(b, lp // SEQ_TILE),
        in_specs=[tile] + [_resident(p.shape) for p in params],
        out_specs=tile,
        out_shape=jax.ShapeDtypeStruct(h.shape, F32),
        scratch_shapes=[
            pltpu.VMEM((CONV_HALO + SEQ_TILE, dc), F32),
            pltpu.VMEM((POOL_HALO + SEQ_TILE, dp), F32),
            pltpu.VMEM((SEQ_TILE, dp), BF16),
            pltpu.VMEM((SEQ_TILE, dc + dp), BF16),
        ],
        compiler_params=pltpu.CompilerParams(
            dimension_semantics=("arbitrary", "arbitrary"), vmem_limit_bytes=VMEM_LIMIT),
        name="even_mixer",
    )(h, *params)


def _chunk_cumsum(x):
    row = lax.broadcasted_iota(jnp.int32, (x.shape[0], 1), 0)
    d = 1
    while d < x.shape[0]:
        x = x + jnp.where(row >= d, pltpu.roll(x, d, axis=0), 0.0)
        d *= 2
    return x


def _odd_kernel(h_ref, g_ref, win_ref, lb_ref, gng_ref, wout_ref, o_ref,
                q_ref, f_ref, v_ref, gate_ref, qt_ref, qm_ref, km_ref, kd_ref, y_ref, st_ref, decay_ref,
                ostate_ref, *, l_real):
    tl, d = h_ref.shape
    hd = HGRN_HEAD_DIM
    heads = d // hd
    l = pl.program_id(1)

    @pl.when(l == 0)
    def _():
        st_ref[...] = jnp.zeros(st_ref.shape, F32)

    x = h_ref[...]
    n = _rmsnorm(x, g_ref[...]).astype(BF16)
    q = _dot(n, win_ref[:, 0:d])
    q_ref[...] = q * _sigmoid(q)
    f_ref[...] = _dot(n, win_ref[:, d:2 * d])
    v_ref[...] = _dot(n, win_ref[:, 2 * d:3 * d]).astype(BF16)
    gt = _dot(n, win_ref[:, 3 * d:4 * d])
    gate_ref[...] = (gt * _sigmoid(gt)).astype(BF16)

    causal = (lax.broadcasted_iota(jnp.int32, (CHUNK, CHUNK), 0)
              >= lax.broadcasted_iota(jnp.int32, (CHUNK, CHUNK), 1))

    n_chunks = tl // CHUNK

    def chunk_rows(c):
        return pl.ds(pl.multiple_of(c * CHUNK, CHUNK), CHUNK)

    def gates(c):
        lb = lb_ref[...]
        forget = lb + (1.0 - lb) * _sigmoid(f_ref[chunk_rows(c), :])
        return 1.0 - forget, _chunk_cumsum(jnp.log(forget))

    def normalize_and_gate(o, rows, lanes):
        on = o * lax.rsqrt(jnp.mean(o * o, axis=-1, keepdims=True) + EPS) * gng_ref[...]
        y_ref[rows, lanes] = (on * gate_ref[rows, lanes].astype(F32)).astype(BF16)

    def prepare(c):
        rows = chunk_rows(c)
        k, b = gates(c)
        q = q_ref[rows, :]
        mid = b[CHUNK // 2 - 1:CHUNK // 2, :]
        b_last = b[CHUNK - 1:CHUNK, :]
        qt_ref[rows, :] = (q * jnp.exp(b)).astype(BF16)
        qm_ref[rows, :] = (q * jnp.exp(jnp.minimum(b - mid, MAX_EXPONENT))).astype(BF16)
        km_ref[rows, :] = (k * jnp.exp(jnp.minimum(mid - b, MAX_EXPONENT))).astype(BF16)
        kd_ref[rows, :] = (k * jnp.exp(b_last - b)).astype(BF16)
        decay_ref[c] = jnp.broadcast_to(jnp.exp(b_last), decay_ref.shape[1:])
        return jnp.max(jnp.maximum(-mid, mid - b_last))

    def attend(c):
        rows = chunk_rows(c)
        decay = decay_ref[c][0:1, :]
        for hh in range(heads):
            lanes = slice(hh * hd, (hh + 1) * hd)
            vv = v_ref[rows, lanes]
            s = jnp.where(causal, _dot_nt(qm_ref[rows, lanes], km_ref[rows, lanes]), 0.0).astype(BF16)
            st = st_ref[hh]
            o_state = _dot_nt(qt_ref[rows, lanes], st.astype(BF16))
            ostate_ref[:, lanes] = o_state
            st_ref[hh] = st * decay[:, lanes] + _dot_tn(vv, kd_ref[rows, lanes])
            normalize_and_gate(_dot(s, vv) + o_state, rows, lanes)

    def attend_exact(c):
        rows = chunk_rows(c)
        k, b = gates(c)
        q = q_ref[rows, :]
        v = v_ref[rows, :].astype(F32)
        t_idx = lax.broadcasted_iota(jnp.int32, (CHUNK, 1), 0)

        def source_row(s_idx, acc):
            pick = lambda a: jnp.sum(jnp.where(t_idx == s_idx, a, 0.0), axis=0, keepdims=True)
            b_s, k_s, v_s = pick(b), pick(k), pick(v)
            p = jnp.where(t_idx >= s_idx, q * jnp.exp(jnp.minimum(b - b_s, 0.0)) * k_s, 0.0)
            return acc + jnp.concatenate(
                [jnp.sum(p[:, hh * hd:(hh + 1) * hd], axis=-1, keepdims=True) * v_s[:, hh * hd:(hh + 1) * hd]
                 for hh in range(heads)], axis=1)

        o = lax.fori_loop(0, CHUNK, source_row, ostate_ref[...])
        for hh in range(heads):
            lanes = slice(hh * hd, (hh + 1) * hd)
            normalize_and_gate(o[:, lanes], rows, lanes)

    def step(c, carry):
        largest_exponent = prepare(c)
        attend(c)

        @pl.when(largest_exponent > MAX_EXPONENT)
        def _():
            attend_exact(c)

        return carry

    lax.fori_loop(0, n_chunks, step, 0)

    o_ref[...] = _zero_pad_rows(x + _dot(y_ref[...], wout_ref[...]), l * tl, l_real)


def _odd_mixer(h, g, w_in, lb, gn_g, w_out, *, l_real):
    b, lp, d = h.shape
    heads = d // HGRN_HEAD_DIM
    tile = pl.BlockSpec((None, SEQ_TILE, d), lambda i, j: (i, j, 0))
    params = (g, w_in, lb, gn_g, w_out)
    big = lambda dt: pltpu.VMEM((SEQ_TILE, d), dt)
    return pl.pallas_call(
        functools.partial(_odd_kernel, l_real=l_real),
        grid=(b, lp // SEQ_TILE),
        in_specs=[tile] + [_resident(p.shape) for p in params],
        out_specs=tile,
        out_shape=jax.ShapeDtypeStruct(h.shape, F32),
        scratch_shapes=[big(F32), big(F32)] + [big(BF16)] * 7 + [
            pltpu.VMEM((heads, HGRN_HEAD_DIM, HGRN_HEAD_DIM), F32),
            pltpu.VMEM((SEQ_TILE // CHUNK, SUBLANES, d), F32),
            pltpu.VMEM((CHUNK, d), F32)],
        compiler_params=pltpu.CompilerParams(
            dimension_semantics=("arbitrary", "arbitrary"), vmem_limit_bytes=VMEM_LIMIT),
        name="odd_mixer",
    )(h, *params)


def kernel(x, meta_tokens, mix_norm_g, mlp_norm_g, final_norm_g, ev_w_in, ev_conv_w, ev_conv_b, ev_ln_g,
           ev_ln_b, ev_pool_w, ev_pool_b, ev_pool_scale, ev_w_out, od_w_in, od_gnorm_g, od_w_out, lb_param,
           mlp_w1, mlp_w2):
    bn, seq, d = x.shape
    depth = mix_norm_g.shape[0]
    l_real = N_META + seq
    l_pad = -(-l_real // SEQ_TILE) * SEQ_TILE

    meta = jnp.broadcast_to(meta_tokens[None].astype(x.dtype), (bn, N_META, d))
    h = jnp.concatenate([meta, x, jnp.zeros((bn, l_pad - l_real, d), x.dtype)], axis=1)

    lb_all = jnp.cumsum(jax.nn.softmax(lb_param.astype(F32), axis=0), axis=0)
    lb_all = lb_all - lb_all[0]

    row = lambda v: v.reshape(1, -1).astype(F32)
    for layer in range(depth):
        j = layer // 2
        if layer % 2 == 0:
            h = _even_mixer(h, row(mix_norm_g[layer]), ev_w_in[j].astype(BF16), ev_conv_w[j], row(ev_conv_b[j]),
                            row(ev_ln_g[j]), row(ev_ln_b[j]), ev_pool_w[j].astype(BF16), row(ev_pool_b[j]),
                            row(ev_pool_scale[j]), ev_w_out[j].astype(BF16), l_real=l_real)
        else:
            h = _odd_mixer(h, row(mix_norm_g[layer]), od_w_in[j].astype(BF16), row(lb_all[layer]),
                           row(od_gnorm_g[j]), od_w_out[j].astype(BF16), l_real=l_real)
        h = _mlp(h.reshape(bn * l_pad, d), row(mlp_norm_g[layer]), row(final_norm_g),
                 mlp_w1[layer].astype(BF16), mlp_w2[layer].astype(BF16),
                 final=(layer == depth - 1)).reshape(bn, l_pad, d)
    return h[:, N_META:l_real]
```

```python
import functools

import jax
import jax.numpy as jnp
from jax import lax
from jax.experimental import pallas as pl
from jax.experimental.pallas import tpu as pltpu

F32 = jnp.float32
BF16 = jnp.bfloat16

N_META = 16
CONV_WIDTH = 31
POOL_WINDOWS = (2, 4, 8, 16)
HGRN_HEAD_DIM = 128
EPS = 1e-6
LANES = 128
SUBLANES = 8

CHUNK = 64
MAX_EXPONENT = 80.0
L_ALIGN = 1408
EVEN_TILE = 1408
ODD_TILE = 704
MLP_TILE = 768
PIECE = 128
ROW_CHUNK = 64
CONV_HALO = 32
POOL_HALO = 16
FF_CHUNK = 1024
VMEM_LIMIT = 56 * 1024 * 1024

assert max(POOL_WINDOWS) - 1 <= POOL_HALO and CONV_WIDTH - 1 <= CONV_HALO


def _sigmoid(x):
    return 1.0 / (1.0 + jnp.exp(-x))


def _rmsnorm(x, g):
    return x * lax.rsqrt(jnp.mean(x * x, axis=-1, keepdims=True) + EPS) * g


def _dot(a, b):
    return jnp.dot(a, b, preferred_element_type=F32)


def _dot_nt(a, b):
    return lax.dot_general(a, b, (((1,), (1,)), ((), ())), preferred_element_type=F32)


def _dot_tn(a, b):
    return lax.dot_general(a, b, (((0,), (0,)), ((), ())), preferred_element_type=F32)


def _zero_pad_rows(y, first_row, l_real):
    row = first_row + lax.broadcasted_iota(jnp.int32, (y.shape[0], 1), 0)
    return jnp.where(row < l_real, y, 0.0)


def _resident(shape):
    nd = len(shape)
    return pl.BlockSpec(shape, lambda *_: (0,) * nd, pipeline_mode=pl.Buffered(1))


def _mlp_kernel(h_ref, g_ref, fg_ref, w1_ref, w2_ref, o_ref, *, final):
    x = h_ref[...]
    n = _rmsnorm(x, g_ref[...]).astype(BF16)
    acc = x
    for c in range(w1_ref.shape[1] // FF_CHUNK):
        cols = slice(c * FF_CHUNK, (c + 1) * FF_CHUNK)
        hid = _dot(n, w1_ref[:, cols])
        hid = jnp.square(jnp.maximum(hid, 0.0)).astype(BF16)
        acc = acc + _dot(hid, w2_ref[cols, :])
    if final:
        acc = _rmsnorm(acc, fg_ref[...])
    o_ref[...] = acc


def _mlp(h2d, g, fg, w1, w2, *, final):
    rows, d = h2d.shape
    tile = pl.BlockSpec((MLP_TILE, d), lambda i: (i, 0))
    return pl.pallas_call(
        functools.partial(_mlp_kernel, final=final),
        grid=(rows // MLP_TILE,),
        in_specs=[tile, _resident(g.shape), _resident(fg.shape), _resident(w1.shape), _resident(w2.shape)],
        out_specs=tile,
        out_shape=jax.ShapeDtypeStruct(h2d.shape, F32),
        compiler_params=pltpu.CompilerParams(
            dimension_semantics=("arbitrary",), vmem_limit_bytes=VMEM_LIMIT),
        name="mlp_final" if final else "mlp",
    )(h2d, g, fg, w1, w2)


def _even_kernel(h_ref, g_ref, win_ref, cw_ref, cb_ref, lng_ref, lnb_ref, pw_ref, pb_ref, ps_ref,
                 wout_ref, o_ref, a_ext, p_ext, y_buf, *, l_real):
    tl = h_ref.shape[0]
    dc = cw_ref.shape[1]
    gd = pw_ref.shape[1]
    n_pieces = tl // PIECE
    l = pl.program_id(1)

    @pl.when(l == 0)
    def _():
        a_ext[:, 0:CONV_HALO, :] = jnp.zeros((a_ext.shape[0], CONV_HALO, LANES), F32)
        p_ext[:, 0:POOL_HALO, :] = jnp.zeros((p_ext.shape[0], POOL_HALO, LANES), F32)

    @pl.when(l > 0)
    def _():
        a_ext[:, 0:CONV_HALO, :] = a_ext[:, tl:tl + CONV_HALO, :]
        p_ext[:, 0:POOL_HALO, :] = p_ext[:, tl:tl + POOL_HALO, :]

    def piece_start(i):
        return pl.multiple_of(i * PIECE, PIECE)

    def project(i):
        r0 = piece_start(i)
        n = _rmsnorm(h_ref[pl.ds(r0, PIECE), :], g_ref[...]).astype(BF16)
        u = _dot(n, win_ref[...])
        a = u[:, 0:dc] * _sigmoid(u[:, dc:2 * dc])
        for lt in range(dc // LANES):
            a_ext[lt, pl.ds(r0 + CONV_HALO, PIECE), :] = a[:, lt * LANES:(lt + 1) * LANES]
        for gi in range(len(POOL_WINDOWS)):
            p_ext[gi, pl.ds(r0 + POOL_HALO, PIECE), :] = u[:, 2 * dc + gi * gd:2 * dc + (gi + 1) * gd]

    def conv_block(r0):
        parts = []
        for lt in range(dc // LANES):
            lanes = slice(lt * LANES, (lt + 1) * LANES)
            part = jnp.broadcast_to(cb_ref[:, lanes], (ROW_CHUNK, LANES))
            for j in range(CONV_WIDTH):
                off = CONV_HALO - (CONV_WIDTH - 1) + j
                part = part + cw_ref[j:j + 1, lanes] * a_ext[lt, pl.ds(r0 + off, ROW_CHUNK), :]
            parts.append(part)
        acc = jnp.concatenate(parts, axis=1)
        mu = jnp.mean(acc, axis=-1, keepdims=True)
        xc = acc - mu
        yn = xc * lax.rsqrt(jnp.mean(xc * xc, axis=-1, keepdims=True) + EPS) * lng_ref[...] + lnb_ref[...]
        return (yn * _sigmoid(yn)).astype(BF16)

    def pool_block(r0, first_pos):
        pos = first_pos + lax.broadcasted_iota(jnp.int32, (ROW_CHUNK, 1), 0)
        parts = []
        for gi, w in enumerate(POOL_WINDOWS):
            cur = p_ext[gi, pl.ds(r0 + POOL_HALO, ROW_CHUNK), :]
            s = cur
            for j in range(1, w):
                s = s + p_ext[gi, pl.ds(r0 + POOL_HALO - j, ROW_CHUNK), :]
            cnt = jnp.minimum(pos + 1, w).astype(F32)
            parts.append((s / cnt - cur).astype(BF16))
        return jnp.concatenate(parts, axis=1)

    def mix(i):
        r0 = piece_start(i)
        blocks = range(PIECE // ROW_CHUNK)
        ya = jnp.concatenate([conv_block(r0 + k * ROW_CHUNK) for k in blocks], axis=0)
        dd = jnp.concatenate([pool_block(r0 + k * ROW_CHUNK, l * tl + r0 + k * ROW_CHUNK) for k in blocks], axis=0)
        yb = [((_dot(dd[:, gi * gd:(gi + 1) * gd], pw_ref[gi]) + pb_ref[:, gi * gd:(gi + 1) * gd])
               * ps_ref[:, gi * gd:(gi + 1) * gd]).astype(BF16) for gi in range(len(POOL_WINDOWS))]
        y_buf[i % 2] = jnp.concatenate([ya] + yb, axis=1)

    def output(i):
        r0 = piece_start(i)
        y = h_ref[pl.ds(r0, PIECE), :] + _dot(y_buf[i % 2], wout_ref[...])
        o_ref[pl.ds(r0, PIECE), :] = _zero_pad_rows(y, l * tl + r0, l_real)

    project(0)
    mix(0)
    project(1)

    def step(i, carry):
        output(i - 1)
        mix(i)
        project(i + 1)
        return carry

    lax.fori_loop(1, n_pieces - 1, step, 0)
    output(n_pieces - 2)
    mix(n_pieces - 1)
    output(n_pieces - 1)


def _even_mixer(h, g, w_in, conv_w, conv_b, ln_g, ln_b, pool_w, pool_b, pool_scale, w_out, *, l_real):
    b, lp, d = h.shape
    dc = conv_w.shape[1]
    dp = pool_scale.shape[1]
    assert dc % LANES == 0 and pool_w.shape[1:] == (LANES, LANES)
    tile = pl.BlockSpec((None, EVEN_TILE, d), lambda i, j: (i, j, 0))
    params = (g, w_in, conv_w, conv_b, ln_g, ln_b, pool_w, pool_b, pool_scale, w_out)
    return pl.pallas_call(
        functools.partial(_even_kernel, l_real=l_real),
        grid=(b, lp // EVEN_TILE),
        in_specs=[tile] + [_resident(p.shape) for p in params],
        out_specs=tile,
        out_shape=jax.ShapeDtypeStruct(h.shape, F32),
        scratch_shapes=[
            pltpu.VMEM((dc // LANES, CONV_HALO + EVEN_TILE, LANES), F32),
            pltpu.VMEM((len(POOL_WINDOWS), POOL_HALO + EVEN_TILE, LANES), F32),
            pltpu.VMEM((2, PIECE, dc + dp), BF16),
        ],
        compiler_params=pltpu.CompilerParams(
            dimension_semantics=("arbitrary", "arbitrary"), vmem_limit_bytes=VMEM_LIMIT),
        name="even_mixer",
    )(h, *params)


def _chunk_cumsum(x):
    row = lax.broadcasted_iota(jnp.int32, (x.shape[0], 1), 0)
    d = 1
    while d < x.shape[0]:
        x = x + jnp.where(row >= d, pltpu.roll(x, d, axis=0), 0.0)
        d *= 2
    return x


def _odd_kernel(h_ref, g_ref, win_ref, lb_ref, gng_ref, wout_ref, o_ref,
                q_ref, f_ref, v_ref, gate_ref, qt_ref, qm_ref, km_ref, kd_ref, y_ref, st_ref, decay_ref,
                ostate_ref, *, l_real):
    tl, d = h_ref.shape
    hd = HGRN_HEAD_DIM
    heads = d // hd
    l = pl.program_id(1)

    @pl.when(l == 0)
    def _():
        st_ref[...] = jnp.zeros(st_ref.shape, F32)

    x = h_ref[...]
    n = _rmsnorm(x, g_ref[...]).astype(BF16)
    q = _dot(n, win_ref[:, 0:d])
    q_ref[...] = q * _sigmoid(q)
    f_ref[...] = _dot(n, win_ref[:, d:2 * d])
    v_ref[...] = _dot(n, win_ref[:, 2 * d:3 * d]).astype(BF16)
    gt = _dot(n, win_ref[:, 3 * d:4 * d])
    gate_ref[...] = (gt * _sigmoid(gt)).astype(BF16)

    causal = (lax.broadcasted_iota(jnp.int32, (CHUNK, CHUNK), 0)
              >= lax.broadcasted_iota(jnp.int32, (CHUNK, CHUNK), 1))

    n_chunks = tl // CHUNK

    def chunk_rows(c):
        return pl.ds(pl.multiple_of(c * CHUNK, CHUNK), CHUNK)

    def gates(c):
        lb = lb_ref[...]
        forget = lb + (1.0 - lb) * _sigmoid(f_ref[chunk_rows(c), :])
        return 1.0 - forget, _chunk_cumsum(jnp.log(forget))

    def normalize_and_gate(o, rows, lanes):
        on = o * lax.rsqrt(jnp.mean(o * o, axis=-1, keepdims=True) + EPS) * gng_ref[...]
        y_ref[rows, lanes] = (on * gate_ref[rows, lanes].astype(F32)).astype(BF16)

    def prepare(c):
        rows = chunk_rows(c)
        k, b = gates(c)
        q = q_ref[rows, :]
        mid = b[CHUNK // 2 - 1:CHUNK // 2, :]
        b_last = b[CHUNK - 1:CHUNK, :]
        qt_ref[rows, :] = (q * jnp.exp(b)).astype(BF16)
        qm_ref[rows, :] = (q * jnp.exp(jnp.minimum(b - mid, MAX_EXPONENT))).astype(BF16)
        km_ref[rows, :] = (k * jnp.exp(jnp.minimum(mid - b, MAX_EXPONENT))).astype(BF16)
        kd_ref[rows, :] = (k * jnp.exp(b_last - b)).astype(BF16)
        decay_ref[c] = jnp.broadcast_to(jnp.exp(b_last), decay_ref.shape[1:])
        return jnp.max(jnp.maximum(-mid, mid - b_last))

    def attend(c):
        rows = chunk_rows(c)
        decay = decay_ref[c][0:1, :]
        for hh in range(heads):
            lanes = slice(hh * hd, (hh + 1) * hd)
            vv = v_ref[rows, lanes]
            s = jnp.where(causal, _dot_nt(qm_ref[rows, lanes], km_ref[rows, lanes]), 0.0).astype(BF16)
            st = st_ref[hh]
            o_state = _dot_nt(qt_ref[rows, lanes], st.astype(BF16))
            ostate_ref[:, lanes] = o_state
            st_ref[hh] = st * decay[:, lanes] + _dot_tn(vv, kd_ref[rows, lanes])
            normalize_and_gate(_dot(s, vv) + o_state, rows, lanes)

    def attend_exact(c):
        rows = chunk_rows(c)
        k, b = gates(c)
        q = q_ref[rows, :]
        v = v_ref[rows, :].astype(F32)
        t_idx = lax.broadcasted_iota(jnp.int32, (CHUNK, 1), 0)

        def source_row(s_idx, acc):
            pick = lambda a: jnp.sum(jnp.where(t_idx == s_idx, a, 0.0), axis=0, keepdims=True)
            b_s, k_s, v_s = pick(b), pick(k), pick(v)
            p = jnp.where(t_idx >= s_idx, q * jnp.exp(jnp.minimum(b - b_s, 0.0)) * k_s, 0.0)
            return acc + jnp.concatenate(
                [jnp.sum(p[:, hh * hd:(hh + 1) * hd], axis=-1, keepdims=True) * v_s[:, hh * hd:(hh + 1) * hd]
                 for hh in range(heads)], axis=1)

        o = lax.fori_loop(0, CHUNK, source_row, ostate_ref[...])
        for hh in range(heads):
            lanes = slice(hh * hd, (hh + 1) * hd)
            normalize_and_gate(o[:, lanes], rows, lanes)

    def step(c, carry):
        largest_exponent = prepare(c)
        attend(c)

        @pl.when(largest_exponent > MAX_EXPONENT)
        def _():
            attend_exact(c)

        return carry

    lax.fori_loop(0, n_chunks, step, 0)

    o_ref[...] = _zero_pad_rows(x + _dot(y_ref[...], wout_ref[...]), l * tl, l_real)


def _odd_mixer(h, g, w_in, lb, gn_g, w_out, *, l_real):
    b, lp, d = h.shape
    heads = d // HGRN_HEAD_DIM
    tile = pl.BlockSpec((None, ODD_TILE, d), lambda i, j: (i, j, 0))
    params = (g, w_in, lb, gn_g, w_out)
    big = lambda dt: pltpu.VMEM((ODD_TILE, d), dt)
    return pl.pallas_call(
        functools.partial(_odd_kernel, l_real=l_real),
        grid=(b, lp // ODD_TILE),
        in_specs=[tile] + [_resident(p.shape) for p in params],
        out_specs=tile,
        out_shape=jax.ShapeDtypeStruct(h.shape, F32),
        scratch_shapes=[big(F32), big(F32)] + [big(BF16)] * 7 + [
            pltpu.VMEM((heads, HGRN_HEAD_DIM, HGRN_HEAD_DIM), F32),
            pltpu.VMEM((ODD_TILE // CHUNK, SUBLANES, d), F32),
            pltpu.VMEM((CHUNK, d), F32)],
        compiler_params=pltpu.CompilerParams(
            dimension_semantics=("arbitrary", "arbitrary"), vmem_limit_bytes=VMEM_LIMIT),
        name="odd_mixer",
    )(h, *params)


def kernel(x, meta_tokens, mix_norm_g, mlp_norm_g, final_norm_g, ev_w_in, ev_conv_w, ev_conv_b, ev_ln_g,
           ev_ln_b, ev_pool_w, ev_pool_b, ev_pool_scale, ev_w_out, od_w_in, od_gnorm_g, od_w_out, lb_param,
           mlp_w1, mlp_w2):
    bn, seq, d = x.shape
    depth = mix_norm_g.shape[0]
    l_real = N_META + seq
    l_pad = -(-l_real // L_ALIGN) * L_ALIGN

    meta = jnp.broadcast_to(meta_tokens[None].astype(x.dtype), (bn, N_META, d))
    h = jnp.concatenate([meta, x, jnp.zeros((bn, l_pad - l_real, d), x.dtype)], axis=1)

    lb_all = jnp.cumsum(jax.nn.softmax(lb_param.astype(F32), axis=0), axis=0)
    lb_all = lb_all - lb_all[0]

    row = lambda v: v.reshape(1, -1).astype(F32)
    for layer in range(depth):
        j = layer // 2
        if layer % 2 == 0:
            h = _even_mixer(h, row(mix_norm_g[layer]), ev_w_in[j].astype(BF16), ev_conv_w[j], row(ev_conv_b[j]),
                            row(ev_ln_g[j]), row(ev_ln_b[j]), ev_pool_w[j].astype(BF16), row(ev_pool_b[j]),
                            row(ev_pool_scale[j]), ev_w_out[j].astype(BF16), l_real=l_real)
        else:
            h = _odd_mixer(h, row(mix_norm_g[layer]), od_w_in[j].astype(BF16), row(lb_all[layer]),
                           row(od_gnorm_g[j]), od_w_out[j].astype(BF16), l_real=l_real)
        h = _mlp(h.reshape(bn * l_pad, d), row(mlp_norm_g[layer]), row(final_norm_g),
                 mlp_w1[layer].astype(BF16), mlp_w2[layer].astype(BF16),
                 final=(layer == depth - 1)).reshape(bn, l_pad, d)
    return h[:, N_META:l_real]
```

```python
import functools

import jax
import jax.numpy as jnp
from jax import lax
from jax.experimental import pallas as pl
from jax.experimental.pallas import tpu as pltpu

F32 = jnp.float32
BF16 = jnp.bfloat16

N_META = 16
CONV_WIDTH = 31
POOL_WINDOWS = (2, 4, 8, 16)
HGRN_HEAD_DIM = 128
EPS = 1e-6
LANES = 128
SUBLANES = 8

CHUNK = 64
MAX_EXPONENT = 80.0
SEQ_TILE = 832
ROW_CHUNK = 64
CONV_HALO = 32
POOL_HALO = 16
FF_CHUNK = 1024
VMEM_LIMIT = 56 * 1024 * 1024

assert max(POOL_WINDOWS) - 1 <= POOL_HALO and CONV_WIDTH - 1 <= CONV_HALO


def _sigmoid(x):
    return 1.0 / (1.0 + jnp.exp(-x))


def _rmsnorm(x, g):
    return x * lax.rsqrt(jnp.mean(x * x, axis=-1, keepdims=True) + EPS) * g


def _dot(a, b):
    return jnp.dot(a, b, preferred_element_type=F32)


def _dot_nt(a, b):
    return lax.dot_general(a, b, (((1,), (1,)), ((), ())), preferred_element_type=F32)


def _zero_pad_rows(y, first_row, l_real):
    row = first_row + lax.broadcasted_iota(jnp.int32, (y.shape[0], 1), 0)
    return jnp.where(row < l_real, y, 0.0)


def _resident(shape):
    nd = len(shape)
    return pl.BlockSpec(shape, lambda *_: (0,) * nd, pipeline_mode=pl.Buffered(1))


def _mlp_kernel(h_ref, g_ref, fg_ref, w1_ref, w2_ref, o_ref, *, final):
    x = h_ref[...]
    n = _rmsnorm(x, g_ref[...]).astype(BF16)
    acc = x
    for c in range(w1_ref.shape[1] // FF_CHUNK):
        cols = slice(c * FF_CHUNK, (c + 1) * FF_CHUNK)
        hid = _dot(n, w1_ref[:, cols])
        hid = jnp.square(jnp.maximum(hid, 0.0)).astype(BF16)
        acc = acc + _dot(hid, w2_ref[cols, :])
    if final:
        acc = _rmsnorm(acc, fg_ref[...])
    o_ref[...] = acc


def _mlp(h2d, g, fg, w1, w2, *, final):
    rows, d = h2d.shape
    tile = pl.BlockSpec((SEQ_TILE, d), lambda i: (i, 0))
    return pl.pallas_call(
        functools.partial(_mlp_kernel, final=final),
        grid=(rows // SEQ_TILE,),
        in_specs=[tile, _resident(g.shape), _resident(fg.shape), _resident(w1.shape), _resident(w2.shape)],
        out_specs=tile,
        out_shape=jax.ShapeDtypeStruct(h2d.shape, F32),
        compiler_params=pltpu.CompilerParams(
            dimension_semantics=("arbitrary",), vmem_limit_bytes=VMEM_LIMIT),
        name="mlp_final" if final else "mlp",
    )(h2d, g, fg, w1, w2)


def _even_kernel(h_ref, g_ref, win_ref, cw_ref, cb_ref, lng_ref, lnb_ref, pw_ref, pb_ref, ps_ref,
                 wout_ref, o_ref, a_ext, p_ext, d_ref, y_ref, *, l_real):
    tl = h_ref.shape[0]
    dc = cw_ref.shape[1]
    gd = pw_ref.shape[1]
    l = pl.program_id(1)

    x = h_ref[...]
    n = _rmsnorm(x, g_ref[...]).astype(BF16)
    u = _dot(n, win_ref[...])

    @pl.when(l == 0)
    def _():
        a_ext[:, 0:CONV_HALO, :] = jnp.zeros((a_ext.shape[0], CONV_HALO, LANES), F32)
        p_ext[:, 0:POOL_HALO, :] = jnp.zeros((p_ext.shape[0], POOL_HALO, LANES), F32)

    @pl.when(l > 0)
    def _():
        a_ext[:, 0:CONV_HALO, :] = a_ext[:, tl:tl + CONV_HALO, :]
        p_ext[:, 0:POOL_HALO, :] = p_ext[:, tl:tl + POOL_HALO, :]

    a = u[:, 0:dc] * _sigmoid(u[:, dc:2 * dc])
    for lt in range(dc // LANES):
        a_ext[lt, CONV_HALO:CONV_HALO + tl, :] = a[:, lt * LANES:(lt + 1) * LANES]
    for gi in range(len(POOL_WINDOWS)):
        p_ext[gi, POOL_HALO:POOL_HALO + tl, :] = u[:, 2 * dc + gi * gd:2 * dc + (gi + 1) * gd]

    def chunk(r, carry):
        r0 = pl.multiple_of(r * ROW_CHUNK, ROW_CHUNK)
        parts = []
        for lt in range(dc // LANES):
            lanes = slice(lt * LANES, (lt + 1) * LANES)
            part = jnp.broadcast_to(cb_ref[:, lanes], (ROW_CHUNK, LANES))
            for j in range(CONV_WIDTH):
                off = CONV_HALO - (CONV_WIDTH - 1) + j
                part = part + cw_ref[j:j + 1, lanes] * a_ext[lt, pl.ds(r0 + off, ROW_CHUNK), :]
            parts.append(part)
        acc = jnp.concatenate(parts, axis=1)
        mu = jnp.mean(acc, axis=-1, keepdims=True)
        xc = acc - mu
        yn = xc * lax.rsqrt(jnp.mean(xc * xc, axis=-1, keepdims=True) + EPS) * lng_ref[...] + lnb_ref[...]
        y_ref[pl.ds(r0, ROW_CHUNK), 0:dc] = (yn * _sigmoid(yn)).astype(BF16)

        pos = l * tl + r0 + lax.broadcasted_iota(jnp.int32, (ROW_CHUNK, 1), 0)
        for gi, w in enumerate(POOL_WINDOWS):
            cur = p_ext[gi, pl.ds(r0 + POOL_HALO, ROW_CHUNK), :]
            s = cur
            for j in range(1, w):
                s = s + p_ext[gi, pl.ds(r0 + POOL_HALO - j, ROW_CHUNK), :]
            cnt = jnp.minimum(pos + 1, w).astype(F32)
            d_ref[pl.ds(r0, ROW_CHUNK), gi * gd:(gi + 1) * gd] = (s / cnt - cur).astype(BF16)
        return carry

    lax.fori_loop(0, tl // ROW_CHUNK, chunk, 0)

    for gi in range(len(POOL_WINDOWS)):
        lanes = slice(gi * gd, (gi + 1) * gd)
        yb = (_dot(d_ref[:, lanes], pw_ref[gi]) + pb_ref[:, lanes]) * ps_ref[:, lanes]
        y_ref[:, dc + gi * gd:dc + (gi + 1) * gd] = yb.astype(BF16)

    o_ref[...] = _zero_pad_rows(x + _dot(y_ref[...], wout_ref[...]), l * tl, l_real)


def _even_mixer(h, g, w_in, conv_w, conv_b, ln_g, ln_b, pool_w, pool_b, pool_scale, w_out, *, l_real):
    b, lp, d = h.shape
    dc = conv_w.shape[1]
    dp = pool_scale.shape[1]
    assert dc % LANES == 0 and pool_w.shape[1:] == (LANES, LANES)
    tile = pl.BlockSpec((None, SEQ_TILE, d), lambda i, j: (i, j, 0))
    params = (g, w_in, conv_w, conv_b, ln_g, ln_b, pool_w, pool_b, pool_scale, w_out)
    return pl.pallas_call(
        functools.partial(_even_kernel, l_real=l_real),
        grid=(b, lp // SEQ_TILE),
        in_specs=[tile] + [_resident(p.shape) for p in params],
        out_specs=tile,
        out_shape=jax.ShapeDtypeStruct(h.shape, F32),
        scratch_shapes=[
            pltpu.VMEM((dc // LANES, CONV_HALO + SEQ_TILE, LANES), F32),
            pltpu.VMEM((len(POOL_WINDOWS), POOL_HALO + SEQ_TILE, LANES), F32),
            pltpu.VMEM((SEQ_TILE, dp), BF16),
            pltpu.VMEM((SEQ_TILE, dc + dp), BF16),
        ],
        compiler_params=pltpu.CompilerParams(
            dimension_semantics=("arbitrary", "arbitrary"), vmem_limit_bytes=VMEM_LIMIT),
        name="even_mixer",
    )(h, *params)


def _chunk_cumsum(x):
    row = lax.broadcasted_iota(jnp.int32, (x.shape[0], 1), 0)
    d = 1
    while d < x.shape[0]:
        x = x + jnp.where(row >= d, pltpu.roll(x, d, axis=0), 0.0)
        d *= 2
    return x


def _odd_kernel(h_ref, g_ref, win_ref, lb_ref, gng_ref, wout_ref, o_ref,
                q_ref, f_ref, v_ref, gate_ref, qt_ref, qm_ref, km_ref, kd_ref, y_ref, st_ref, decay_ref,
                ostate_ref, *, l_real):
    tl, d = h_ref.shape
    hd = HGRN_HEAD_DIM
    heads = d // hd
    l = pl.program_id(1)

    @pl.when(l == 0)
    def _():
        st_ref[...] = jnp.zeros(st_ref.shape, F32)

    x = h_ref[...]
    n = _rmsnorm(x, g_ref[...]).astype(BF16)
    q = _dot(n, win_ref[:, 0:d])
    q_ref[...] = q * _sigmoid(q)
    f_ref[...] = _dot(n, win_ref[:, d:2 * d])
    v_ref[...] = _dot(n, win_ref[:, 2 * d:3 * d]).astype(BF16)
    gt = _dot(n, win_ref[:, 3 * d:4 * d])
    gate_ref[...] = (gt * _sigmoid(gt)).astype(BF16)

    causal = (lax.broadcasted_iota(jnp.int32, (CHUNK, CHUNK), 0)
              >= lax.broadcasted_iota(jnp.int32, (CHUNK, CHUNK), 1))
    eye = (lax.broadcasted_iota(jnp.int32, (hd, hd), 0)
           == lax.broadcasted_iota(jnp.int32, (hd, hd), 1)).astype(BF16)
    head_lanes = [slice(hh * hd, (hh + 1) * hd) for hh in range(heads)]
    n_chunks = tl // CHUNK

    def chunk_rows(c):
        return pl.ds(pl.multiple_of(c * CHUNK, CHUNK), CHUNK)

    def gates(c):
        lb = lb_ref[...]
        forget = lb + (1.0 - lb) * _sigmoid(f_ref[chunk_rows(c), :])
        return 1.0 - forget, _chunk_cumsum(jnp.log(forget))

    def normalize_and_gate(o, rows, lanes):
        on = o * lax.rsqrt(jnp.mean(o * o, axis=-1, keepdims=True) + EPS) * gng_ref[...]
        y_ref[rows, lanes] = (on * gate_ref[rows, lanes].astype(F32)).astype(BF16)

    def prepare(c):
        rows = chunk_rows(c)
        k, b = gates(c)
        q = q_ref[rows, :]
        mid = b[CHUNK // 2 - 1:CHUNK // 2, :]
        b_last = b[CHUNK - 1:CHUNK, :]
        qt_ref[rows, :] = (q * jnp.exp(b)).astype(BF16)
        qm_ref[rows, :] = (q * jnp.exp(jnp.minimum(b - mid, MAX_EXPONENT))).astype(BF16)
        km_ref[rows, :] = (k * jnp.exp(jnp.minimum(mid - b, MAX_EXPONENT))).astype(BF16)
        kd_ref[rows, :] = (k * jnp.exp(b_last - b)).astype(BF16)
        decay_ref[c] = jnp.broadcast_to(jnp.exp(b_last), decay_ref.shape[1:])
        return jnp.max(jnp.maximum(-mid, mid - b_last))

    def attend(c):
        rows = chunk_rows(c)
        decay = decay_ref[c][0:1, :]
        scores = [jnp.where(causal, _dot_nt(qm_ref[rows, ln], km_ref[rows, ln]), 0.0).astype(BF16)
                  for ln in head_lanes]
        v_t = [_dot_nt(eye, v_ref[rows, ln]).astype(BF16) for ln in head_lanes]
        o_state = [_dot_nt(qt_ref[rows, ln], st_ref[hh].astype(BF16)) for hh, ln in enumerate(head_lanes)]
        o_intra = [_dot(scores[hh], v_ref[rows, ln]) for hh, ln in enumerate(head_lanes)]
        incr = [_dot(v_t[hh], kd_ref[rows, ln]) for hh, ln in enumerate(head_lanes)]
        for hh, ln in enumerate(head_lanes):
            ostate_ref[:, ln] = o_state[hh]
            st_ref[hh] = st_ref[hh] * decay[:, ln] + incr[hh]
            normalize_and_gate(o_intra[hh] + o_state[hh], rows, ln)

    def attend_exact(c):
        rows = chunk_rows(c)
        k, b = gates(c)
        q = q_ref[rows, :]
        v = v_ref[rows, :].astype(F32)
        t_idx = lax.broadcasted_iota(jnp.int32, (CHUNK, 1), 0)

        def source_row(s_idx, acc):
            pick = lambda a: jnp.sum(jnp.where(t_idx == s_idx, a, 0.0), axis=0, keepdims=True)
            b_s, k_s, v_s = pick(b), pick(k), pick(v)
            w = jnp.where(t_idx >= s_idx, q * jnp.exp(jnp.minimum(b - b_s, 0.0)) * k_s, 0.0)
            return acc + jnp.concatenate(
                [jnp.sum(w[:, ln], axis=-1, keepdims=True) * v_s[:, ln] for ln in head_lanes], axis=1)

        o = lax.fori_loop(0, CHUNK, source_row, ostate_ref[...])
        for ln in head_lanes:
            normalize_and_gate(o[:, ln], rows, ln)

    def step(c, carry):
        largest_exponent = prepare(c)
        attend(c)

        @pl.when(largest_exponent > MAX_EXPONENT)
        def _():
            attend_exact(c)

        return carry

    lax.fori_loop(0, n_chunks, step, 0)

    o_ref[...] = _zero_pad_rows(x + _dot(y_ref[...], wout_ref[...]), l * tl, l_real)


def _odd_mixer(h, g, w_in, lb, gn_g, w_out, *, l_real):
    b, lp, d = h.shape
    heads = d // HGRN_HEAD_DIM
    tile = pl.BlockSpec((None, SEQ_TILE, d), lambda i, j: (i, j, 0))
    params = (g, w_in, lb, gn_g, w_out)
    big = lambda dt: pltpu.VMEM((SEQ_TILE, d), dt)
    return pl.pallas_call(
        functools.partial(_odd_kernel, l_real=l_real),
        grid=(b, lp // SEQ_TILE),
        in_specs=[tile] + [_resident(p.shape) for p in params],
        out_specs=tile,
        out_shape=jax.ShapeDtypeStruct(h.shape, F32),
        scratch_shapes=[big(F32), big(F32)] + [big(BF16)] * 7 + [
            pltpu.VMEM((heads, HGRN_HEAD_DIM, HGRN_HEAD_DIM), F32),
            pltpu.VMEM((SEQ_TILE // CHUNK, SUBLANES, d), F32),
            pltpu.VMEM((CHUNK, d), F32)],
        compiler_params=pltpu.CompilerParams(
            dimension_semantics=("arbitrary", "arbitrary"), vmem_limit_bytes=VMEM_LIMIT),
        name="odd_mixer",
    )(h, *params)


def kernel(x, meta_tokens, mix_norm_g, mlp_norm_g, final_norm_g, ev_w_in, ev_conv_w, ev_conv_b, ev_ln_g,
           ev_ln_b, ev_pool_w, ev_pool_b, ev_pool_scale, ev_w_out, od_w_in, od_gnorm_g, od_w_out, lb_param,
           mlp_w1, mlp_w2):
    bn, seq, d = x.shape
    depth = mix_norm_g.shape[0]
    l_real = N_META + seq
    l_pad = -(-l_real // SEQ_TILE) * SEQ_TILE

    meta = jnp.broadcast_to(meta_tokens[None].astype(x.dtype), (bn, N_META, d))
    h = jnp.concatenate([meta, x, jnp.zeros((bn, l_pad - l_real, d), x.dtype)], axis=1)

    lb_all = jnp.cumsum(jax.nn.softmax(lb_param.astype(F32), axis=0), axis=0)
    lb_all = lb_all - lb_all[0]

    row = lambda v: v.reshape(1, -1).astype(F32)
    for layer in range(depth):
        j = layer // 2
        if layer % 2 == 0:
            h = _even_mixer(h, row(mix_norm_g[layer]), ev_w_in[j].astype(BF16), ev_conv_w[j], row(ev_conv_b[j]),
                            row(ev_ln_g[j]), row(ev_ln_b[j]), ev_pool_w[j].astype(BF16), row(ev_pool_b[j]),
                            row(ev_pool_scale[j]), ev_w_out[j].astype(BF16), l_real=l_real)
        else:
            h = _odd_mixer(h, row(mix_norm_g[layer]), od_w_in[j].astype(BF16), row(lb_all[layer]),
                           row(od_gnorm_g[j]), od_w_out[j].astype(BF16), l_real=l_real)
        h = _mlp(h.reshape(bn * l_pad, d), row(mlp_norm_g[layer]), row(final_norm_g),
                 mlp_w1[layer].astype(BF16), mlp_w2[layer].astype(BF16),
                 final=(layer == depth - 1)).reshape(bn, l_pad, d)
    return h[:, N_META:l_real]
```

```python
import functools

import jax
import jax.numpy as jnp
from jax import lax
from jax.experimental import pallas as pl
from jax.experimental.pallas import tpu as pltpu

F32 = jnp.float32
BF16 = jnp.bfloat16

N_META = 16
CONV_WIDTH = 31
POOL_WINDOWS = (2, 4, 8, 16)
HGRN_HEAD_DIM = 128
EPS = 1e-6
LANES = 128
SUBLANES = 8

CHUNK = 64
MAX_EXPONENT = 80.0
SEQ_TILE = 832
ROW_CHUNK = 64
CONV_HALO = 32
POOL_HALO = 16
FF_CHUNK = 1024
VMEM_LIMIT = 56 * 1024 * 1024

assert max(POOL_WINDOWS) - 1 <= POOL_HALO and CONV_WIDTH - 1 <= CONV_HALO


def _sigmoid(x):
    return 1.0 / (1.0 + jnp.exp(-x))


def _rmsnorm(x, g):
    return x * lax.rsqrt(jnp.mean(x * x, axis=-1, keepdims=True) + EPS) * g


def _dot(a, b):
    return jnp.dot(a, b, preferred_element_type=F32)


def _dot_nt(a, b):
    return lax.dot_general(a, b, (((1,), (1,)), ((), ())), preferred_element_type=F32)


def _zero_pad_rows(y, first_row, l_real):
    row = first_row + lax.broadcasted_iota(jnp.int32, (y.shape[0], 1), 0)
    return jnp.where(row < l_real, y, 0.0)


def _resident(shape):
    nd = len(shape)
    return pl.BlockSpec(shape, lambda *_: (0,) * nd, pipeline_mode=pl.Buffered(1))


def _mlp_rows(x, g_ref, w1_ref, w2_ref):
    n = _rmsnorm(x, g_ref[...]).astype(BF16)
    acc = x
    for c in range(w1_ref.shape[1] // FF_CHUNK):
        cols = slice(c * FF_CHUNK, (c + 1) * FF_CHUNK)
        hid = _dot(n, w1_ref[:, cols])
        hid = jnp.square(jnp.maximum(hid, 0.0)).astype(BF16)
        acc = acc + _dot(hid, w2_ref[cols, :])
    return acc


def _mlp_kernel(h_ref, g_ref, w1_ref, w2_ref, o_ref):
    o_ref[...] = _mlp_rows(h_ref[...], g_ref, w1_ref, w2_ref)


def _mlp(h2d, g, w1, w2):
    rows, d = h2d.shape
    tile = pl.BlockSpec((SEQ_TILE, d), lambda i: (i, 0))
    return pl.pallas_call(
        _mlp_kernel,
        grid=(rows // SEQ_TILE,),
        in_specs=[tile, _resident(g.shape), _resident(w1.shape), _resident(w2.shape)],
        out_specs=tile,
        out_shape=jax.ShapeDtypeStruct(h2d.shape, F32),
        compiler_params=pltpu.CompilerParams(
            dimension_semantics=("arbitrary",), vmem_limit_bytes=VMEM_LIMIT),
        name="mlp",
    )(h2d, g, w1, w2)


def _mlp_final_kernel(h_ref, hnext_ref, g_ref, fg_ref, w1_ref, w2_ref, o_ref):
    x = jnp.concatenate([h_ref[N_META:, :], hnext_ref[0:N_META, :]], axis=0)
    o_ref[...] = _rmsnorm(_mlp_rows(x, g_ref, w1_ref, w2_ref), fg_ref[...])


def _mlp_final(h, g, fg, w1, w2, *, seq):
    b, lp, d = h.shape
    last = lp // SEQ_TILE - 1
    return pl.pallas_call(
        _mlp_final_kernel,
        grid=(b, pl.cdiv(seq, SEQ_TILE)),
        in_specs=[pl.BlockSpec((None, SEQ_TILE, d), lambda i, j: (i, j, 0)),
                  pl.BlockSpec((None, SEQ_TILE, d), lambda i, j: (i, jnp.minimum(j + 1, last), 0)),
                  _resident(g.shape), _resident(fg.shape), _resident(w1.shape), _resident(w2.shape)],
        out_specs=pl.BlockSpec((None, SEQ_TILE, d), lambda i, j: (i, j, 0)),
        out_shape=jax.ShapeDtypeStruct((b, seq, d), F32),
        compiler_params=pltpu.CompilerParams(
            dimension_semantics=("arbitrary", "arbitrary"), vmem_limit_bytes=VMEM_LIMIT),
        name="mlp_final",
    )(h, h, g, fg, w1, w2)


def _even_kernel(*refs, l_real, from_inputs):
    if from_inputs:
        meta_ref, xprev_ref, xcur_ref = refs[:3]
        refs = refs[2:]
    (h_ref, g_ref, win_ref, cw_ref, cb_ref, lng_ref, lnb_ref, pw_ref, pb_ref, ps_ref, wout_ref, o_ref,
     a_ext, p_ext, d_ref, y_ref) = refs
    tl = o_ref.shape[0]
    dc = cw_ref.shape[1]
    gd = pw_ref.shape[1]
    l = pl.program_id(1)

    if from_inputs:
        head = jnp.where(l == 0, meta_ref[...], xprev_ref[tl - N_META:, :])
        x = _zero_pad_rows(jnp.concatenate([head, xcur_ref[0:tl - N_META, :]], axis=0), l * tl, l_real)
    else:
        x = h_ref[...]
    n = _rmsnorm(x, g_ref[...]).astype(BF16)
    u = _dot(n, win_ref[...])

    @pl.when(l == 0)
    def _():
        a_ext[:, 0:CONV_HALO, :] = jnp.zeros((a_ext.shape[0], CONV_HALO, LANES), F32)
        p_ext[:, 0:POOL_HALO, :] = jnp.zeros((p_ext.shape[0], POOL_HALO, LANES), F32)

    @pl.when(l > 0)
    def _():
        a_ext[:, 0:CONV_HALO, :] = a_ext[:, tl:tl + CONV_HALO, :]
        p_ext[:, 0:POOL_HALO, :] = p_ext[:, tl:tl + POOL_HALO, :]

    a = u[:, 0:dc] * _sigmoid(u[:, dc:2 * dc])
    for lt in range(dc // LANES):
        a_ext[lt, CONV_HALO:CONV_HALO + tl, :] = a[:, lt * LANES:(lt + 1) * LANES]
    for gi in range(len(POOL_WINDOWS)):
        p_ext[gi, POOL_HALO:POOL_HALO + tl, :] = u[:, 2 * dc + gi * gd:2 * dc + (gi + 1) * gd]

    def chunk(r, carry):
        r0 = pl.multiple_of(r * ROW_CHUNK, ROW_CHUNK)
        parts = []
        for lt in range(dc // LANES):
            lanes = slice(lt * LANES, (lt + 1) * LANES)
            part = jnp.broadcast_to(cb_ref[:, lanes], (ROW_CHUNK, LANES))
            for j in range(CONV_WIDTH):
                off = CONV_HALO - (CONV_WIDTH - 1) + j
                part = part + cw_ref[j:j + 1, lanes] * a_ext[lt, pl.ds(r0 + off, ROW_CHUNK), :]
            parts.append(part)
        acc = jnp.concatenate(parts, axis=1)
        mu = jnp.mean(acc, axis=-1, keepdims=True)
        xc = acc - mu
        yn = xc * lax.rsqrt(jnp.mean(xc * xc, axis=-1, keepdims=True) + EPS) * lng_ref[...] + lnb_ref[...]
        y_ref[pl.ds(r0, ROW_CHUNK), 0:dc] = (yn * _sigmoid(yn)).astype(BF16)

        pos = l * tl + r0 + lax.broadcasted_iota(jnp.int32, (ROW_CHUNK, 1), 0)
        for gi, w in enumerate(POOL_WINDOWS):
            cur = p_ext[gi, pl.ds(r0 + POOL_HALO, ROW_CHUNK), :]
            s = cur
            for j in range(1, w):
                s = s + p_ext[gi, pl.ds(r0 + POOL_HALO - j, ROW_CHUNK), :]
            cnt = jnp.minimum(pos + 1, w).astype(F32)
            d_ref[pl.ds(r0, ROW_CHUNK), gi * gd:(gi + 1) * gd] = (s / cnt - cur).astype(BF16)
        return carry

    lax.fori_loop(0, tl // ROW_CHUNK, chunk, 0)

    for gi in range(len(POOL_WINDOWS)):
        lanes = slice(gi * gd, (gi + 1) * gd)
        yb = (_dot(d_ref[:, lanes], pw_ref[gi]) + pb_ref[:, lanes]) * ps_ref[:, lanes]
        y_ref[:, dc + gi * gd:dc + (gi + 1) * gd] = yb.astype(BF16)

    o_ref[...] = _zero_pad_rows(x + _dot(y_ref[...], wout_ref[...]), l * tl, l_real)


def _even_mixer(h, g, w_in, conv_w, conv_b, ln_g, ln_b, pool_w, pool_b, pool_scale, w_out, *, l_real, l_pad,
                meta=None):
    b, _, d = h.shape
    dc = conv_w.shape[1]
    dp = pool_scale.shape[1]
    assert dc % LANES == 0 and pool_w.shape[1:] == (LANES, LANES)
    tile = pl.BlockSpec((None, SEQ_TILE, d), lambda i, j: (i, j, 0))
    params = (g, w_in, conv_w, conv_b, ln_g, ln_b, pool_w, pool_b, pool_scale, w_out)
    if meta is None:
        stream_in, stream_specs = (h,), [tile]
    else:
        prev = pl.BlockSpec((None, SEQ_TILE, d), lambda i, j: (i, jnp.maximum(j - 1, 0), 0))
        stream_in, stream_specs = (meta, h, h), [_resident(meta.shape), prev, tile]
    return pl.pallas_call(
        functools.partial(_even_kernel, l_real=l_real, from_inputs=meta is not None),
        grid=(b, l_pad // SEQ_TILE),
        in_specs=stream_specs + [_resident(p.shape) for p in params],
        out_specs=tile,
        out_shape=jax.ShapeDtypeStruct((b, l_pad, d), F32),
        scratch_shapes=[
            pltpu.VMEM((dc // LANES, CONV_HALO + SEQ_TILE, LANES), F32),
            pltpu.VMEM((len(POOL_WINDOWS), POOL_HALO + SEQ_TILE, LANES), F32),
            pltpu.VMEM((SEQ_TILE, dp), BF16),
            pltpu.VMEM((SEQ_TILE, dc + dp), BF16),
        ],
        compiler_params=pltpu.CompilerParams(
            dimension_semantics=("arbitrary", "arbitrary"), vmem_limit_bytes=VMEM_LIMIT),
        name="even_mixer",
    )(*stream_in, *params)


def _chunk_cumsum(x):
    row = lax.broadcasted_iota(jnp.int32, (x.shape[0], 1), 0)
    d = 1
    while d < x.shape[0]:
        x = x + jnp.where(row >= d, pltpu.roll(x, d, axis=0), 0.0)
        d *= 2
    return x


def _odd_kernel(h_ref, g_ref, win_ref, lb_ref, gng_ref, wout_ref, o_ref,
                q_ref, f_ref, v_ref, gate_ref, qt_ref, qm_ref, km_ref, kd_ref, y_ref, st_ref, decay_ref,
                ostate_ref, *, l_real):
    tl, d = h_ref.shape
    hd = HGRN_HEAD_DIM
    heads = d // hd
    l = pl.program_id(1)

    @pl.when(l == 0)
    def _():
        st_ref[...] = jnp.zeros(st_ref.shape, F32)

    x = h_ref[...]
    n = _rmsnorm(x, g_ref[...]).astype(BF16)
    q = _dot(n, win_ref[:, 0:d])
    q_ref[...] = q * _sigmoid(q)
    f_ref[...] = _dot(n, win_ref[:, d:2 * d])
    v_ref[...] = _dot(n, win_ref[:, 2 * d:3 * d]).astype(BF16)
    gt = _dot(n, win_ref[:, 3 * d:4 * d])
    gate_ref[...] = (gt * _sigmoid(gt)).astype(BF16)

    causal = (lax.broadcasted_iota(jnp.int32, (CHUNK, CHUNK), 0)
              >= lax.broadcasted_iota(jnp.int32, (CHUNK, CHUNK), 1))
    eye = (lax.broadcasted_iota(jnp.int32, (hd, hd), 0)
           == lax.broadcasted_iota(jnp.int32, (hd, hd), 1)).astype(BF16)
    head_lanes = [slice(hh * hd, (hh + 1) * hd) for hh in range(heads)]
    n_chunks = tl // CHUNK

    def chunk_rows(c):
        return pl.ds(pl.multiple_of(c * CHUNK, CHUNK), CHUNK)

    def gates(c):
        lb = lb_ref[...]
        forget = lb + (1.0 - lb) * _sigmoid(f_ref[chunk_rows(c), :])
        return 1.0 - forget, _chunk_cumsum(jnp.log(forget))

    def normalize_and_gate(o, rows, lanes):
        on = o * lax.rsqrt(jnp.mean(o * o, axis=-1, keepdims=True) + EPS) * gng_ref[...]
        y_ref[rows, lanes] = (on * gate_ref[rows, lanes].astype(F32)).astype(BF16)

    def prepare(c):
        rows = chunk_rows(c)
        k, b = gates(c)
        q = q_ref[rows, :]
        mid = b[CHUNK // 2 - 1:CHUNK // 2, :]
        b_last = b[CHUNK - 1:CHUNK, :]
        qt_ref[rows, :] = (q * jnp.exp(b)).astype(BF16)
        qm_ref[rows, :] = (q * jnp.exp(jnp.minimum(b - mid, MAX_EXPONENT))).astype(BF16)
        km_ref[rows, :] = (k * jnp.exp(jnp.minimum(mid - b, MAX_EXPONENT))).astype(BF16)
        kd_ref[rows, :] = (k * jnp.exp(b_last - b)).astype(BF16)
        decay_ref[c] = jnp.broadcast_to(jnp.exp(b_last), decay_ref.shape[1:])
        return jnp.max(jnp.maximum(-mid, mid - b_last))

    def attend(c):
        rows = chunk_rows(c)
        decay = decay_ref[c][0:1, :]
        scores = [jnp.where(causal, _dot_nt(qm_ref[rows, ln], km_ref[rows, ln]), 0.0).astype(BF16)
                  for ln in head_lanes]
        v_t = [_dot_nt(eye, v_ref[rows, ln]).astype(BF16) for ln in head_lanes]
        o_state = [_dot_nt(qt_ref[rows, ln], st_ref[hh].astype(BF16)) for hh, ln in enumerate(head_lanes)]
        o_intra = [_dot(scores[hh], v_ref[rows, ln]) for hh, ln in enumerate(head_lanes)]
        incr = [_dot(v_t[hh], kd_ref[rows, ln]) for hh, ln in enumerate(head_lanes)]
        for hh, ln in enumerate(head_lanes):
            ostate_ref[:, ln] = o_state[hh]
            st_ref[hh] = st_ref[hh] * decay[:, ln] + incr[hh]
            normalize_and_gate(o_intra[hh] + o_state[hh], rows, ln)

    def attend_exact(c):
        rows = chunk_rows(c)
        k, b = gates(c)
        q = q_ref[rows, :]
        v = v_ref[rows, :].astype(F32)
        t_idx = lax.broadcasted_iota(jnp.int32, (CHUNK, 1), 0)

        def source_row(s_idx, acc):
            pick = lambda a: jnp.sum(jnp.where(t_idx == s_idx, a, 0.0), axis=0, keepdims=True)
            b_s, k_s, v_s = pick(b), pick(k), pick(v)
            w = jnp.where(t_idx >= s_idx, q * jnp.exp(jnp.minimum(b - b_s, 0.0)) * k_s, 0.0)
            return acc + jnp.concatenate(
                [jnp.sum(w[:, ln], axis=-1, keepdims=True) * v_s[:, ln] for ln in head_lanes], axis=1)

        o = lax.fori_loop(0, CHUNK, source_row, ostate_ref[...])
        for ln in head_lanes:
            normalize_and_gate(o[:, ln], rows, ln)

    def step(c, carry):
        largest_exponent = prepare(c)
        attend(c)

        @pl.when(largest_exponent > MAX_EXPONENT)
        def _():
            attend_exact(c)

        return carry

    lax.fori_loop(0, n_chunks, step, 0)

    o_ref[...] = _zero_pad_rows(x + _dot(y_ref[...], wout_ref[...]), l * tl, l_real)


def _odd_mixer(h, g, w_in, lb, gn_g, w_out, *, l_real):
    b, lp, d = h.shape
    heads = d // HGRN_HEAD_DIM
    tile = pl.BlockSpec((None, SEQ_TILE, d), lambda i, j: (i, j, 0))
    params = (g, w_in, lb, gn_g, w_out)
    big = lambda dt: pltpu.VMEM((SEQ_TILE, d), dt)
    return pl.pallas_call(
        functools.partial(_odd_kernel, l_real=l_real),
        grid=(b, lp // SEQ_TILE),
        in_specs=[tile] + [_resident(p.shape) for p in params],
        out_specs=tile,
        out_shape=jax.ShapeDtypeStruct(h.shape, F32),
        scratch_shapes=[big(F32), big(F32)] + [big(BF16)] * 7 + [
            pltpu.VMEM((heads, HGRN_HEAD_DIM, HGRN_HEAD_DIM), F32),
            pltpu.VMEM((SEQ_TILE // CHUNK, SUBLANES, d), F32),
            pltpu.VMEM((CHUNK, d), F32)],
        compiler_params=pltpu.CompilerParams(
            dimension_semantics=("arbitrary", "arbitrary"), vmem_limit_bytes=VMEM_LIMIT),
        name="odd_mixer",
    )(h, *params)


def kernel(x, meta_tokens, mix_norm_g, mlp_norm_g, final_norm_g, ev_w_in, ev_conv_w, ev_conv_b, ev_ln_g,
           ev_ln_b, ev_pool_w, ev_pool_b, ev_pool_scale, ev_w_out, od_w_in, od_gnorm_g, od_w_out, lb_param,
           mlp_w1, mlp_w2):
    bn, seq, d = x.shape
    depth = mix_norm_g.shape[0]
    l_real = N_META + seq
    l_pad = -(-l_real // SEQ_TILE) * SEQ_TILE

    lb_all = jnp.cumsum(jax.nn.softmax(lb_param.astype(F32), axis=0), axis=0)
    lb_all = lb_all - lb_all[0]

    row = lambda v: v.reshape(1, -1).astype(F32)
    h = x.astype(F32)
    for layer in range(depth):
        j = layer // 2
        if layer % 2 == 0:
            h = _even_mixer(h, row(mix_norm_g[layer]), ev_w_in[j].astype(BF16), ev_conv_w[j], row(ev_conv_b[j]),
                            row(ev_ln_g[j]), row(ev_ln_b[j]), ev_pool_w[j].astype(BF16), row(ev_pool_b[j]),
                            row(ev_pool_scale[j]), ev_w_out[j].astype(BF16), l_real=l_real, l_pad=l_pad,
                            meta=meta_tokens.astype(F32) if layer == 0 else None)
        else:
            h = _odd_mixer(h, row(mix_norm_g[layer]), od_w_in[j].astype(BF16), row(lb_all[layer]),
                           row(od_gnorm_g[j]), od_w_out[j].astype(BF16), l_real=l_real)
        w1, w2 = mlp_w1[layer].astype(BF16), mlp_w2[layer].astype(BF16)
        if layer == depth - 1:
            return _mlp_final(h, row(mlp_norm_g[layer]), row(final_norm_g), w1, w2, seq=seq)
        h = _mlp(h.reshape(bn * l_pad, d), row(mlp_norm_g[layer]), w1, w2).reshape(bn, l_pad, d)
```

```python
import functools

import jax
import jax.numpy as jnp
from jax import lax
from jax.experimental import pallas as pl
from jax.experimental.pallas import tpu as pltpu

F32 = jnp.float32
BF16 = jnp.bfloat16

N_META = 16
CONV_WIDTH = 31
POOL_WINDOWS = (2, 4, 8, 16)
HGRN_HEAD_DIM = 128
EPS = 1e-6
LANES = 128
SUBLANES = 8

CHUNK = 64
MAX_EXPONENT = 80.0
SEQ_TILE = 832
ROW_CHUNK = 64
CONV_HALO = 32
POOL_HALO = 16
FF_CHUNK = 1024
VMEM_LIMIT = 56 * 1024 * 1024

assert max(POOL_WINDOWS) - 1 <= POOL_HALO and CONV_WIDTH - 1 <= CONV_HALO


def _sigmoid(x):
    return 1.0 / (1.0 + jnp.exp(-x))


def _rmsnorm(x, g):
    return x * lax.rsqrt(jnp.mean(x * x, axis=-1, keepdims=True) + EPS) * g


def _dot(a, b):
    return jnp.dot(a, b, preferred_element_type=F32)


def _dot_nt(a, b):
    return lax.dot_general(a, b, (((1,), (1,)), ((), ())), preferred_element_type=F32)


def _zero_pad_rows(y, first_row, l_real):
    row = first_row + lax.broadcasted_iota(jnp.int32, (y.shape[0], 1), 0)
    return jnp.where(row < l_real, y, 0.0)


def _resident(shape):
    nd = len(shape)
    return pl.BlockSpec(shape, lambda *_: (0,) * nd, pipeline_mode=pl.Buffered(1))


def _layer_of(stacked, layer):
    return stacked, layer


def _operand(p):
    return p[0] if isinstance(p, tuple) else p


def _param_spec(p):
    if not isinstance(p, tuple):
        return _resident(p.shape)
    stacked, layer = p
    nd = stacked.ndim
    return pl.BlockSpec((None,) + stacked.shape[1:], lambda *_: (layer,) + (0,) * (nd - 1),
                        pipeline_mode=pl.Buffered(1))


def _param_shape(p):
    return p[0].shape[1:] if isinstance(p, tuple) else p.shape


def _mlp_rows(x, g_ref, w1_ref, w2_ref):
    n = _rmsnorm(x, g_ref[...]).astype(BF16)
    acc = x
    for c in range(w1_ref.shape[1] // FF_CHUNK):
        cols = slice(c * FF_CHUNK, (c + 1) * FF_CHUNK)
        hid = _dot(n, w1_ref[:, cols])
        hid = jnp.square(jnp.maximum(hid, 0.0)).astype(BF16)
        acc = acc + _dot(hid, w2_ref[cols, :])
    return acc


def _mlp_kernel(h_ref, g_ref, w1_ref, w2_ref, o_ref):
    o_ref[...] = _mlp_rows(h_ref[...], g_ref, w1_ref, w2_ref)


def _mlp(h2d, g, w1, w2):
    rows, d = h2d.shape
    tile = pl.BlockSpec((SEQ_TILE, d), lambda i: (i, 0))
    return pl.pallas_call(
        _mlp_kernel,
        grid=(rows // SEQ_TILE,),
        in_specs=[tile, _resident(g.shape), _param_spec(w1), _param_spec(w2)],
        out_specs=tile,
        out_shape=jax.ShapeDtypeStruct(h2d.shape, F32),
        compiler_params=pltpu.CompilerParams(
            dimension_semantics=("arbitrary",), vmem_limit_bytes=VMEM_LIMIT),
        name="mlp",
    )(h2d, g, _operand(w1), _operand(w2))


def _mlp_final_kernel(h_ref, hnext_ref, g_ref, fg_ref, w1_ref, w2_ref, o_ref):
    x = jnp.concatenate([h_ref[N_META:, :], hnext_ref[0:N_META, :]], axis=0)
    o_ref[...] = _rmsnorm(_mlp_rows(x, g_ref, w1_ref, w2_ref), fg_ref[...])


def _mlp_final(h, g, fg, w1, w2, *, seq):
    b, lp, d = h.shape
    last = lp // SEQ_TILE - 1
    return pl.pallas_call(
        _mlp_final_kernel,
        grid=(b, pl.cdiv(seq, SEQ_TILE)),
        in_specs=[pl.BlockSpec((None, SEQ_TILE, d), lambda i, j: (i, j, 0)),
                  pl.BlockSpec((None, SEQ_TILE, d), lambda i, j: (i, jnp.minimum(j + 1, last), 0)),
                  _resident(g.shape), _resident(fg.shape), _param_spec(w1), _param_spec(w2)],
        out_specs=pl.BlockSpec((None, SEQ_TILE, d), lambda i, j: (i, j, 0)),
        out_shape=jax.ShapeDtypeStruct((b, seq, d), F32),
        compiler_params=pltpu.CompilerParams(
            dimension_semantics=("arbitrary", "arbitrary"), vmem_limit_bytes=VMEM_LIMIT),
        name="mlp_final",
    )(h, h, g, fg, _operand(w1), _operand(w2))


def _even_kernel(*refs, l_real, from_inputs):
    if from_inputs:
        meta_ref, xprev_ref, xcur_ref = refs[:3]
        refs = refs[2:]
    (h_ref, g_ref, win_ref, cw_ref, cb_ref, lng_ref, lnb_ref, pw_ref, pb_ref, ps_ref, wout_ref, o_ref,
     a_ext, p_ext, d_ref, y_ref) = refs
    tl = o_ref.shape[0]
    dc = cw_ref.shape[1]
    gd = pw_ref.shape[1]
    l = pl.program_id(1)

    if from_inputs:
        head = jnp.where(l == 0, meta_ref[...], xprev_ref[tl - N_META:, :])
        x = _zero_pad_rows(jnp.concatenate([head, xcur_ref[0:tl - N_META, :]], axis=0), l * tl, l_real)
    else:
        x = h_ref[...]
    n = _rmsnorm(x, g_ref[...]).astype(BF16)
    u = _dot(n, win_ref[...])

    @pl.when(l == 0)
    def _():
        a_ext[:, 0:CONV_HALO, :] = jnp.zeros((a_ext.shape[0], CONV_HALO, LANES), F32)
        p_ext[:, 0:POOL_HALO, :] = jnp.zeros((p_ext.shape[0], POOL_HALO, LANES), F32)

    @pl.when(l > 0)
    def _():
        a_ext[:, 0:CONV_HALO, :] = a_ext[:, tl:tl + CONV_HALO, :]
        p_ext[:, 0:POOL_HALO, :] = p_ext[:, tl:tl + POOL_HALO, :]

    a = u[:, 0:dc] * _sigmoid(u[:, dc:2 * dc])
    for lt in range(dc // LANES):
        a_ext[lt, CONV_HALO:CONV_HALO + tl, :] = a[:, lt * LANES:(lt + 1) * LANES]
    for gi in range(len(POOL_WINDOWS)):
        p_ext[gi, POOL_HALO:POOL_HALO + tl, :] = u[:, 2 * dc + gi * gd:2 * dc + (gi + 1) * gd]

    def chunk(r, carry):
        r0 = pl.multiple_of(r * ROW_CHUNK, ROW_CHUNK)
        parts = []
        for lt in range(dc // LANES):
            lanes = slice(lt * LANES, (lt + 1) * LANES)
            part = jnp.broadcast_to(cb_ref[:, lanes], (ROW_CHUNK, LANES))
            for j in range(CONV_WIDTH):
                off = CONV_HALO - (CONV_WIDTH - 1) + j
                part = part + cw_ref[j:j + 1, lanes] * a_ext[lt, pl.ds(r0 + off, ROW_CHUNK), :]
            parts.append(part)
        acc = jnp.concatenate(parts, axis=1)
        mu = jnp.mean(acc, axis=-1, keepdims=True)
        xc = acc - mu
        yn = xc * lax.rsqrt(jnp.mean(xc * xc, axis=-1, keepdims=True) + EPS) * lng_ref[...] + lnb_ref[...]
        y_ref[pl.ds(r0, ROW_CHUNK), 0:dc] = (yn * _sigmoid(yn)).astype(BF16)

        pos = l * tl + r0 + lax.broadcasted_iota(jnp.int32, (ROW_CHUNK, 1), 0)
        for gi, w in enumerate(POOL_WINDOWS):
            cur = p_ext[gi, pl.ds(r0 + POOL_HALO, ROW_CHUNK), :]
            s = cur
            for j in range(1, w):
                s = s + p_ext[gi, pl.ds(r0 + POOL_HALO - j, ROW_CHUNK), :]
            cnt = jnp.minimum(pos + 1, w).astype(F32)
            d_ref[pl.ds(r0, ROW_CHUNK), gi * gd:(gi + 1) * gd] = (s / cnt - cur).astype(BF16)
        return carry

    lax.fori_loop(0, tl // ROW_CHUNK, chunk, 0)

    for gi in range(len(POOL_WINDOWS)):
        lanes = slice(gi * gd, (gi + 1) * gd)
        yb = (_dot(d_ref[:, lanes], pw_ref[gi]) + pb_ref[:, lanes]) * ps_ref[:, lanes]
        y_ref[:, dc + gi * gd:dc + (gi + 1) * gd] = yb.astype(BF16)

    o_ref[...] = _zero_pad_rows(x + _dot(y_ref[...], wout_ref[...]), l * tl, l_real)


def _even_mixer(h, g, w_in, conv_w, conv_b, ln_g, ln_b, pool_w, pool_b, pool_scale, w_out, *, l_real, l_pad,
                meta=None):
    b, _, d = h.shape
    dc = _param_shape(conv_w)[1]
    dp = pool_scale.shape[1]
    assert dc % LANES == 0 and _param_shape(pool_w)[1:] == (LANES, LANES)
    tile = pl.BlockSpec((None, SEQ_TILE, d), lambda i, j: (i, j, 0))
    params = (g, w_in, conv_w, conv_b, ln_g, ln_b, pool_w, pool_b, pool_scale, w_out)
    if meta is None:
        stream_in, stream_specs = (h,), [tile]
    else:
        prev = pl.BlockSpec((None, SEQ_TILE, d), lambda i, j: (i, jnp.maximum(j - 1, 0), 0))
        stream_in, stream_specs = (meta, h, h), [_resident(meta.shape), prev, tile]
    return pl.pallas_call(
        functools.partial(_even_kernel, l_real=l_real, from_inputs=meta is not None),
        grid=(b, l_pad // SEQ_TILE),
        in_specs=stream_specs + [_param_spec(p) for p in params],
        out_specs=tile,
        out_shape=jax.ShapeDtypeStruct((b, l_pad, d), F32),
        scratch_shapes=[
            pltpu.VMEM((dc // LANES, CONV_HALO + SEQ_TILE, LANES), F32),
            pltpu.VMEM((len(POOL_WINDOWS), POOL_HALO + SEQ_TILE, LANES), F32),
            pltpu.VMEM((SEQ_TILE, dp), BF16),
            pltpu.VMEM((SEQ_TILE, dc + dp), BF16),
        ],
        compiler_params=pltpu.CompilerParams(
            dimension_semantics=("arbitrary", "arbitrary"), vmem_limit_bytes=VMEM_LIMIT),
        name="even_mixer",
    )(*stream_in, *map(_operand, params))


def _chunk_cumsum(x):
    row = lax.broadcasted_iota(jnp.int32, (x.shape[0], 1), 0)
    d = 1
    while d < x.shape[0]:
        x = x + jnp.where(row >= d, pltpu.roll(x, d, axis=0), 0.0)
        d *= 2
    return x


def _odd_kernel(h_ref, g_ref, win_ref, lb_ref, gng_ref, wout_ref, o_ref,
                q_ref, f_ref, v_ref, gate_ref, qt_ref, qm_ref, km_ref, kd_ref, y_ref, st_ref, decay_ref,
                ostate_ref, *, l_real):
    tl, d = h_ref.shape
    hd = HGRN_HEAD_DIM
    heads = d // hd
    l = pl.program_id(1)

    @pl.when(l == 0)
    def _():
        st_ref[...] = jnp.zeros(st_ref.shape, F32)

    x = h_ref[...]
    n = _rmsnorm(x, g_ref[...]).astype(BF16)
    q = _dot(n, win_ref[:, 0:d])
    q_ref[...] = q * _sigmoid(q)
    f_ref[...] = _dot(n, win_ref[:, d:2 * d])
    v_ref[...] = _dot(n, win_ref[:, 2 * d:3 * d]).astype(BF16)
    gt = _dot(n, win_ref[:, 3 * d:4 * d])
    gate_ref[...] = (gt * _sigmoid(gt)).astype(BF16)

    causal = (lax.broadcasted_iota(jnp.int32, (CHUNK, CHUNK), 0)
              >= lax.broadcasted_iota(jnp.int32, (CHUNK, CHUNK), 1))
    eye = (lax.broadcasted_iota(jnp.int32, (hd, hd), 0)
           == lax.broadcasted_iota(jnp.int32, (hd, hd), 1)).astype(BF16)
    head_lanes = [slice(hh * hd, (hh + 1) * hd) for hh in range(heads)]
    n_chunks = tl // CHUNK

    def chunk_rows(c):
        return pl.ds(pl.multiple_of(c * CHUNK, CHUNK), CHUNK)

    def gates(c):
        lb = lb_ref[...]
        forget = lb + (1.0 - lb) * _sigmoid(f_ref[chunk_rows(c), :])
        return 1.0 - forget, _chunk_cumsum(jnp.log(forget))

    def normalize_and_gate(o, rows, lanes):
        on = o * lax.rsqrt(jnp.mean(o * o, axis=-1, keepdims=True) + EPS) * gng_ref[...]
        y_ref[rows, lanes] = (on * gate_ref[rows, lanes].astype(F32)).astype(BF16)

    def prepare(c):
        rows = chunk_rows(c)
        k, b = gates(c)
        q = q_ref[rows, :]
        mid = b[CHUNK // 2 - 1:CHUNK // 2, :]
        b_last = b[CHUNK - 1:CHUNK, :]
        qt_ref[rows, :] = (q * jnp.exp(b)).astype(BF16)
        qm_ref[rows, :] = (q * jnp.exp(jnp.minimum(b - mid, MAX_EXPONENT))).astype(BF16)
        km_ref[rows, :] = (k * jnp.exp(jnp.minimum(mid - b, MAX_EXPONENT))).astype(BF16)
        kd_ref[rows, :] = (k * jnp.exp(b_last - b)).astype(BF16)
        decay_ref[c] = jnp.broadcast_to(jnp.exp(b_last), decay_ref.shape[1:])
        return jnp.max(jnp.maximum(-mid, mid - b_last))

    def attend(c):
        rows = chunk_rows(c)
        decay = decay_ref[c][0:1, :]
        scores = [jnp.where(causal, _dot_nt(qm_ref[rows, ln], km_ref[rows, ln]), 0.0).astype(BF16)
                  for ln in head_lanes]
        v_t = [_dot_nt(eye, v_ref[rows, ln]).astype(BF16) for ln in head_lanes]
        o_state = [_dot_nt(qt_ref[rows, ln], st_ref[hh].astype(BF16)) for hh, ln in enumerate(head_lanes)]
        o_intra = [_dot(scores[hh], v_ref[rows, ln]) for hh, ln in enumerate(head_lanes)]
        incr = [_dot(v_t[hh], kd_ref[rows, ln]) for hh, ln in enumerate(head_lanes)]
        for hh, ln in enumerate(head_lanes):
            ostate_ref[:, ln] = o_state[hh]
            st_ref[hh] = st_ref[hh] * decay[:, ln] + incr[hh]
            normalize_and_gate(o_intra[hh] + o_state[hh], rows, ln)

    def attend_exact(c):
        rows = chunk_rows(c)
        k, b = gates(c)
        q = q_ref[rows, :]
        v = v_ref[rows, :].astype(F32)
        t_idx = lax.broadcasted_iota(jnp.int32, (CHUNK, 1), 0)

        def source_row(s_idx, acc):
            pick = lambda a: jnp.sum(jnp.where(t_idx == s_idx, a, 0.0), axis=0, keepdims=True)
            b_s, k_s, v_s = pick(b), pick(k), pick(v)
            w = jnp.where(t_idx >= s_idx, q * jnp.exp(jnp.minimum(b - b_s, 0.0)) * k_s, 0.0)
            return acc + jnp.concatenate(
                [jnp.sum(w[:, ln], axis=-1, keepdims=True) * v_s[:, ln] for ln in head_lanes], axis=1)

        o = lax.fori_loop(0, CHUNK, source_row, ostate_ref[...])
        for ln in head_lanes:
            normalize_and_gate(o[:, ln], rows, ln)

    def step(c, carry):
        largest_exponent = prepare(c)
        attend(c)

        @pl.when(largest_exponent > MAX_EXPONENT)
        def _():
            attend_exact(c)

        return carry

    lax.fori_loop(0, n_chunks, step, 0)

    o_ref[...] = _zero_pad_rows(x + _dot(y_ref[...], wout_ref[...]), l * tl, l_real)


def _odd_mixer(h, g, w_in, lb, gn_g, w_out, *, l_real):
    b, lp, d = h.shape
    heads = d // HGRN_HEAD_DIM
    tile = pl.BlockSpec((None, SEQ_TILE, d), lambda i, j: (i, j, 0))
    params = (g, w_in, lb, gn_g, w_out)
    big = lambda dt: pltpu.VMEM((SEQ_TILE, d), dt)
    return pl.pallas_call(
        functools.partial(_odd_kernel, l_real=l_real),
        grid=(b, lp // SEQ_TILE),
        in_specs=[tile] + [_param_spec(p) for p in params],
        out_specs=tile,
        out_shape=jax.ShapeDtypeStruct(h.shape, F32),
        scratch_shapes=[big(F32), big(F32)] + [big(BF16)] * 7 + [
            pltpu.VMEM((heads, HGRN_HEAD_DIM, HGRN_HEAD_DIM), F32),
            pltpu.VMEM((SEQ_TILE // CHUNK, SUBLANES, d), F32),
            pltpu.VMEM((CHUNK, d), F32)],
        compiler_params=pltpu.CompilerParams(
            dimension_semantics=("arbitrary", "arbitrary"), vmem_limit_bytes=VMEM_LIMIT),
        name="odd_mixer",
    )(h, *map(_operand, params))


def kernel(x, meta_tokens, mix_norm_g, mlp_norm_g, final_norm_g, ev_w_in, ev_conv_w, ev_conv_b, ev_ln_g,
           ev_ln_b, ev_pool_w, ev_pool_b, ev_pool_scale, ev_w_out, od_w_in, od_gnorm_g, od_w_out, lb_param,
           mlp_w1, mlp_w2):
    bn, seq, d = x.shape
    depth = mix_norm_g.shape[0]
    l_real = N_META + seq
    l_pad = -(-l_real // SEQ_TILE) * SEQ_TILE

    lb_all = jnp.cumsum(jax.nn.softmax(lb_param.astype(F32), axis=0), axis=0)
    lb_all = lb_all - lb_all[0]

    row = lambda v: v.reshape(1, -1).astype(F32)
    ev_w_in, ev_pool_w, ev_w_out, od_w_in, od_w_out, mlp_w1, mlp_w2 = (
        w.astype(BF16) for w in (ev_w_in, ev_pool_w, ev_w_out, od_w_in, od_w_out, mlp_w1, mlp_w2))
    h = x.astype(F32)
    for layer in range(depth):
        j = layer // 2
        if layer % 2 == 0:
            h = _even_mixer(h, row(mix_norm_g[layer]), _layer_of(ev_w_in, j), _layer_of(ev_conv_w, j),
                            row(ev_conv_b[j]), row(ev_ln_g[j]), row(ev_ln_b[j]), _layer_of(ev_pool_w, j),
                            row(ev_pool_b[j]), row(ev_pool_scale[j]), _layer_of(ev_w_out, j),
                            l_real=l_real, l_pad=l_pad, meta=meta_tokens.astype(F32) if layer == 0 else None)
        else:
            h = _odd_mixer(h, row(mix_norm_g[layer]), _layer_of(od_w_in, j), row(lb_all[layer]),
                           row(od_gnorm_g[j]), _layer_of(od_w_out, j), l_real=l_real)
        w1, w2 = _layer_of(mlp_w1, layer), _layer_of(mlp_w2, layer)
        if layer == depth - 1:
            return _mlp_final(h, row(mlp_norm_g[layer]), row(final_norm_g), w1, w2, seq=seq)
        h = _mlp(h.reshape(bn * l_pad, d), row(mlp_norm_g[layer]), w1, w2).reshape(bn, l_pad, d)
```

```python
import functools

import jax
import jax.numpy as jnp
from jax import lax
from jax.experimental import pallas as pl
from jax.experimental.pallas import tpu as pltpu

F32 = jnp.float32
BF16 = jnp.bfloat16

N_META = 16
CONV_WIDTH = 31
POOL_WINDOWS = (2, 4, 8, 16)
HGRN_HEAD_DIM = 128
EPS = 1e-6
LANES = 128
SUBLANES = 8

CHUNK = 64
MAX_EXPONENT = 80.0
SEQ_TILE = 832
ROW_CHUNK = 64
CONV_HALO = 32
POOL_HALO = 16
FF_CHUNK = 1024
VMEM_LIMIT = 56 * 1024 * 1024

assert max(POOL_WINDOWS) - 1 <= POOL_HALO and CONV_WIDTH - 1 <= CONV_HALO


def _sigmoid(x):
    return 1.0 / (1.0 + jnp.exp(-x))


def _rmsnorm(x, g):
    return x * lax.rsqrt(jnp.mean(x * x, axis=-1, keepdims=True) + EPS) * g


def _dot(a, b):
    return jnp.dot(a, b, preferred_element_type=F32)


def _dot_nt(a, b):
    return lax.dot_general(a, b, (((1,), (1,)), ((), ())), preferred_element_type=F32)


def _zero_pad_rows(y, first_row, l_real):
    row = first_row + lax.broadcasted_iota(jnp.int32, (y.shape[0], 1), 0)
    return jnp.where(row < l_real, y, 0.0)


def _resident(shape):
    nd = len(shape)
    return pl.BlockSpec(shape, lambda *_: (0,) * nd, pipeline_mode=pl.Buffered(1))


def _layer_of(stacked, layer):
    return stacked, layer


def _operand(p):
    return p[0] if isinstance(p, tuple) else p


def _param_spec(p):
    if not isinstance(p, tuple):
        return _resident(p.shape)
    stacked, layer = p
    nd = stacked.ndim
    return pl.BlockSpec((None,) + stacked.shape[1:], lambda *_: (layer,) + (0,) * (nd - 1),
                        pipeline_mode=pl.Buffered(1))


def _param_shape(p):
    return p[0].shape[1:] if isinstance(p, tuple) else p.shape


def _mlp_rows(x, g_ref, w1_ref, w2_ref):
    n = _rmsnorm(x, g_ref[...]).astype(BF16)
    acc = x
    for c in range(w1_ref.shape[1] // FF_CHUNK):
        cols = slice(c * FF_CHUNK, (c + 1) * FF_CHUNK)
        hid = _dot(n, w1_ref[:, cols])
        hid = jnp.square(jnp.maximum(hid, 0.0)).astype(BF16)
        acc = acc + _dot(hid, w2_ref[cols, :])
    return acc


def _mlp_kernel(h_ref, g_ref, w1_ref, w2_ref, o_ref):
    o_ref[...] = _mlp_rows(h_ref[...], g_ref, w1_ref, w2_ref)


def _mlp(h2d, g, w1, w2):
    rows, d = h2d.shape
    tile = pl.BlockSpec((SEQ_TILE, d), lambda i: (i, 0))
    return pl.pallas_call(
        _mlp_kernel,
        grid=(rows // SEQ_TILE,),
        in_specs=[tile, _resident(g.shape), _param_spec(w1), _param_spec(w2)],
        out_specs=tile,
        out_shape=jax.ShapeDtypeStruct(h2d.shape, F32),
        compiler_params=pltpu.CompilerParams(
            dimension_semantics=("arbitrary",), vmem_limit_bytes=VMEM_LIMIT),
        name="mlp",
    )(h2d, g, _operand(w1), _operand(w2))


def _mlp_final_kernel(h_ref, hnext_ref, g_ref, fg_ref, w1_ref, w2_ref, o_ref):
    x = jnp.concatenate([h_ref[N_META:, :], hnext_ref[0:N_META, :]], axis=0)
    o_ref[...] = _rmsnorm(_mlp_rows(x, g_ref, w1_ref, w2_ref), fg_ref[...])


def _mlp_final(h, g, fg, w1, w2, *, seq):
    b, lp, d = h.shape
    last = lp // SEQ_TILE - 1
    return pl.pallas_call(
        _mlp_final_kernel,
        grid=(b, pl.cdiv(seq, SEQ_TILE)),
        in_specs=[pl.BlockSpec((None, SEQ_TILE, d), lambda i, j: (i, j, 0)),
                  pl.BlockSpec((None, SEQ_TILE, d), lambda i, j: (i, jnp.minimum(j + 1, last), 0)),
                  _resident(g.shape), _resident(fg.shape), _param_spec(w1), _param_spec(w2)],
        out_specs=pl.BlockSpec((None, SEQ_TILE, d), lambda i, j: (i, j, 0)),
        out_shape=jax.ShapeDtypeStruct((b, seq, d), F32),
        compiler_params=pltpu.CompilerParams(
            dimension_semantics=("arbitrary", "arbitrary"), vmem_limit_bytes=VMEM_LIMIT),
        name="mlp_final",
    )(h, h, g, fg, _operand(w1), _operand(w2))


def _even_kernel(*refs, l_real, from_inputs):
    if from_inputs:
        meta_ref, xprev_ref, xcur_ref = refs[:3]
        refs = refs[2:]
    (h_ref, g_ref, win_ref, cw_ref, cb_ref, lng_ref, lnb_ref, pw_ref, pb_ref, ps_ref, wout_ref, o_ref,
     a_ext, p_ext, conv_ref, d_ref, y_ref) = refs
    tl = o_ref.shape[0]
    dc = cw_ref.shape[1]
    gd = pw_ref.shape[1]
    l = pl.program_id(1)

    if from_inputs:
        head = jnp.where(l == 0, meta_ref[...], xprev_ref[tl - N_META:, :])
        x = _zero_pad_rows(jnp.concatenate([head, xcur_ref[0:tl - N_META, :]], axis=0), l * tl, l_real)
    else:
        x = h_ref[...]
    n = _rmsnorm(x, g_ref[...]).astype(BF16)
    u = _dot(n, win_ref[...])

    @pl.when(l == 0)
    def _():
        a_ext[:, 0:CONV_HALO, :] = jnp.zeros((a_ext.shape[0], CONV_HALO, LANES), F32)
        p_ext[:, 0:POOL_HALO, :] = jnp.zeros((p_ext.shape[0], POOL_HALO, LANES), F32)

    @pl.when(l > 0)
    def _():
        a_ext[:, 0:CONV_HALO, :] = a_ext[:, tl:tl + CONV_HALO, :]
        p_ext[:, 0:POOL_HALO, :] = p_ext[:, tl:tl + POOL_HALO, :]

    a = u[:, 0:dc] * _sigmoid(u[:, dc:2 * dc])
    for lt in range(dc // LANES):
        a_ext[lt, CONV_HALO:CONV_HALO + tl, :] = a[:, lt * LANES:(lt + 1) * LANES]
    for gi in range(len(POOL_WINDOWS)):
        p_ext[gi, POOL_HALO:POOL_HALO + tl, :] = u[:, 2 * dc + gi * gd:2 * dc + (gi + 1) * gd]

    def chunk_rows(r):
        return pl.multiple_of(r * ROW_CHUNK, ROW_CHUNK)

    def accumulate(r):
        r0 = chunk_rows(r)
        for lt in range(dc // LANES):
            lanes = slice(lt * LANES, (lt + 1) * LANES)
            part = jnp.broadcast_to(cb_ref[:, lanes], (ROW_CHUNK, LANES))
            for j in range(CONV_WIDTH):
                off = CONV_HALO - (CONV_WIDTH - 1) + j
                part = part + cw_ref[j:j + 1, lanes] * a_ext[lt, pl.ds(r0 + off, ROW_CHUNK), :]
            conv_ref[pl.ds(r0, ROW_CHUNK), lanes] = part

        pos = l * tl + r0 + lax.broadcasted_iota(jnp.int32, (ROW_CHUNK, 1), 0)
        for gi, w in enumerate(POOL_WINDOWS):
            cur = p_ext[gi, pl.ds(r0 + POOL_HALO, ROW_CHUNK), :]
            s = cur
            for j in range(1, w):
                s = s + p_ext[gi, pl.ds(r0 + POOL_HALO - j, ROW_CHUNK), :]
            cnt = jnp.minimum(pos + 1, w).astype(F32)
            d_ref[pl.ds(r0, ROW_CHUNK), gi * gd:(gi + 1) * gd] = (s / cnt - cur).astype(BF16)

    def normalize(r):
        rows = pl.ds(chunk_rows(r), ROW_CHUNK)
        acc = conv_ref[rows, :]
        mu = jnp.mean(acc, axis=-1, keepdims=True)
        xc = acc - mu
        yn = xc * lax.rsqrt(jnp.mean(xc * xc, axis=-1, keepdims=True) + EPS) * lng_ref[...] + lnb_ref[...]
        y_ref[rows, 0:dc] = (yn * _sigmoid(yn)).astype(BF16)

    def chunk(r, carry):
        accumulate(r)
        normalize(r - 1)
        return carry

    n_row_chunks = tl // ROW_CHUNK
    accumulate(0)
    lax.fori_loop(1, n_row_chunks, chunk, 0)
    normalize(n_row_chunks - 1)

    for gi in range(len(POOL_WINDOWS)):
        lanes = slice(gi * gd, (gi + 1) * gd)
        yb = (_dot(d_ref[:, lanes], pw_ref[gi]) + pb_ref[:, lanes]) * ps_ref[:, lanes]
        y_ref[:, dc + gi * gd:dc + (gi + 1) * gd] = yb.astype(BF16)

    o_ref[...] = _zero_pad_rows(x + _dot(y_ref[...], wout_ref[...]), l * tl, l_real)


def _even_mixer(h, g, w_in, conv_w, conv_b, ln_g, ln_b, pool_w, pool_b, pool_scale, w_out, *, l_real, l_pad,
                meta=None):
    b, _, d = h.shape
    dc = _param_shape(conv_w)[1]
    dp = pool_scale.shape[1]
    assert dc % LANES == 0 and _param_shape(pool_w)[1:] == (LANES, LANES)
    tile = pl.BlockSpec((None, SEQ_TILE, d), lambda i, j: (i, j, 0))
    params = (g, w_in, conv_w, conv_b, ln_g, ln_b, pool_w, pool_b, pool_scale, w_out)
    if meta is None:
        stream_in, stream_specs = (h,), [tile]
    else:
        prev = pl.BlockSpec((None, SEQ_TILE, d), lambda i, j: (i, jnp.maximum(j - 1, 0), 0))
        stream_in, stream_specs = (meta, h, h), [_resident(meta.shape), prev, tile]
    return pl.pallas_call(
        functools.partial(_even_kernel, l_real=l_real, from_inputs=meta is not None),
        grid=(b, l_pad // SEQ_TILE),
        in_specs=stream_specs + [_param_spec(p) for p in params],
        out_specs=tile,
        out_shape=jax.ShapeDtypeStruct((b, l_pad, d), F32),
        scratch_shapes=[
            pltpu.VMEM((dc // LANES, CONV_HALO + SEQ_TILE, LANES), F32),
            pltpu.VMEM((len(POOL_WINDOWS), POOL_HALO + SEQ_TILE, LANES), F32),
            pltpu.VMEM((SEQ_TILE, dc), F32),
            pltpu.VMEM((SEQ_TILE, dp), BF16),
            pltpu.VMEM((SEQ_TILE, dc + dp), BF16),
        ],
        compiler_params=pltpu.CompilerParams(
            dimension_semantics=("arbitrary", "arbitrary"), vmem_limit_bytes=VMEM_LIMIT),
        name="even_mixer",
    )(*stream_in, *map(_operand, params))


def _chunk_cumsum(x):
    row = lax.broadcasted_iota(jnp.int32, (x.shape[0], 1), 0)
    d = 1
    while d < x.shape[0]:
        x = x + jnp.where(row >= d, pltpu.roll(x, d, axis=0), 0.0)
        d *= 2
    return x


def _odd_kernel(h_ref, g_ref, win_ref, lb_ref, gng_ref, wout_ref, o_ref,
                q_ref, f_ref, v_ref, gate_ref, qt_ref, qm_ref, km_ref, kd_ref, y_ref, st_ref, decay_ref,
                ostate_ref, opre_ref, *, l_real):
    tl, d = h_ref.shape
    hd = HGRN_HEAD_DIM
    heads = d // hd
    l = pl.program_id(1)

    @pl.when(l == 0)
    def _():
        st_ref[...] = jnp.zeros(st_ref.shape, F32)

    x = h_ref[...]
    n = _rmsnorm(x, g_ref[...]).astype(BF16)
    q = _dot(n, win_ref[:, 0:d])
    q_ref[...] = q * _sigmoid(q)
    f_ref[...] = _dot(n, win_ref[:, d:2 * d])
    v_ref[...] = _dot(n, win_ref[:, 2 * d:3 * d]).astype(BF16)
    gt = _dot(n, win_ref[:, 3 * d:4 * d])
    gate_ref[...] = (gt * _sigmoid(gt)).astype(BF16)

    causal = (lax.broadcasted_iota(jnp.int32, (CHUNK, CHUNK), 0)
              >= lax.broadcasted_iota(jnp.int32, (CHUNK, CHUNK), 1))
    eye = (lax.broadcasted_iota(jnp.int32, (hd, hd), 0)
           == lax.broadcasted_iota(jnp.int32, (hd, hd), 1)).astype(BF16)
    head_lanes = [slice(hh * hd, (hh + 1) * hd) for hh in range(heads)]
    n_chunks = tl // CHUNK

    def chunk_rows(c):
        return pl.ds(pl.multiple_of(c * CHUNK, CHUNK), CHUNK)

    def gates(c):
        lb = lb_ref[...]
        forget = lb + (1.0 - lb) * _sigmoid(f_ref[chunk_rows(c), :])
        return 1.0 - forget, _chunk_cumsum(jnp.log(forget))

    def normalize_and_gate(c):
        rows = chunk_rows(c)
        for ln in head_lanes:
            o = opre_ref[c % 2, :, ln]
            on = o * lax.rsqrt(jnp.mean(o * o, axis=-1, keepdims=True) + EPS) * gng_ref[...]
            y_ref[rows, ln] = (on * gate_ref[rows, ln].astype(F32)).astype(BF16)

    def prepare(c):
        rows = chunk_rows(c)
        k, b = gates(c)
        q = q_ref[rows, :]
        mid = b[CHUNK // 2 - 1:CHUNK // 2, :]
        b_last = b[CHUNK - 1:CHUNK, :]
        qt_ref[rows, :] = (q * jnp.exp(b)).astype(BF16)
        qm_ref[rows, :] = (q * jnp.exp(jnp.minimum(b - mid, MAX_EXPONENT))).astype(BF16)
        km_ref[rows, :] = (k * jnp.exp(jnp.minimum(mid - b, MAX_EXPONENT))).astype(BF16)
        kd_ref[rows, :] = (k * jnp.exp(b_last - b)).astype(BF16)
        decay_ref[c] = jnp.broadcast_to(jnp.exp(b_last), decay_ref.shape[1:])
        return jnp.max(jnp.maximum(-mid, mid - b_last))

    def attend(c):
        rows = chunk_rows(c)
        decay = decay_ref[c][0:1, :]
        scores = [jnp.where(causal, _dot_nt(qm_ref[rows, ln], km_ref[rows, ln]), 0.0).astype(BF16)
                  for ln in head_lanes]
        v_t = [_dot_nt(eye, v_ref[rows, ln]).astype(BF16) for ln in head_lanes]
        o_state = [_dot_nt(qt_ref[rows, ln], st_ref[hh].astype(BF16)) for hh, ln in enumerate(head_lanes)]
        o_intra = [_dot(scores[hh], v_ref[rows, ln]) for hh, ln in enumerate(head_lanes)]
        incr = [_dot(v_t[hh], kd_ref[rows, ln]) for hh, ln in enumerate(head_lanes)]
        for hh, ln in enumerate(head_lanes):
            ostate_ref[:, ln] = o_state[hh]
            st_ref[hh] = st_ref[hh] * decay[:, ln] + incr[hh]
            opre_ref[c % 2, :, ln] = o_intra[hh] + o_state[hh]

    def attend_exact(c):
        rows = chunk_rows(c)
        k, b = gates(c)
        q = q_ref[rows, :]
        v = v_ref[rows, :].astype(F32)
        t_idx = lax.broadcasted_iota(jnp.int32, (CHUNK, 1), 0)

        def source_row(s_idx, acc):
            pick = lambda a: jnp.sum(jnp.where(t_idx == s_idx, a, 0.0), axis=0, keepdims=True)
            b_s, k_s, v_s = pick(b), pick(k), pick(v)
            w = jnp.where(t_idx >= s_idx, q * jnp.exp(jnp.minimum(b - b_s, 0.0)) * k_s, 0.0)
            return acc + jnp.concatenate(
                [jnp.sum(w[:, ln], axis=-1, keepdims=True) * v_s[:, ln] for ln in head_lanes], axis=1)

        opre_ref[c % 2] = lax.fori_loop(0, CHUNK, source_row, ostate_ref[...])

    def step(c, behind):
        largest_exponent = prepare(c)
        if behind:
            normalize_and_gate(c - 1)
        attend(c)

        @pl.when(largest_exponent > MAX_EXPONENT)
        def _():
            attend_exact(c)

    def body(c, carry):
        step(c, True)
        return carry

    step(0, False)
    lax.fori_loop(1, n_chunks, body, 0)
    normalize_and_gate(n_chunks - 1)

    o_ref[...] = _zero_pad_rows(x + _dot(y_ref[...], wout_ref[...]), l * tl, l_real)


def _odd_mixer(h, g, w_in, lb, gn_g, w_out, *, l_real):
    b, lp, d = h.shape
    heads = d // HGRN_HEAD_DIM
    tile = pl.BlockSpec((None, SEQ_TILE, d), lambda i, j: (i, j, 0))
    params = (g, w_in, lb, gn_g, w_out)
    big = lambda dt: pltpu.VMEM((SEQ_TILE, d), dt)
    return pl.pallas_call(
        functools.partial(_odd_kernel, l_real=l_real),
        grid=(b, lp // SEQ_TILE),
        in_specs=[tile] + [_param_spec(p) for p in params],
        out_specs=tile,
        out_shape=jax.ShapeDtypeStruct(h.shape, F32),
        scratch_shapes=[big(F32), big(F32)] + [big(BF16)] * 7 + [
            pltpu.VMEM((heads, HGRN_HEAD_DIM, HGRN_HEAD_DIM), F32),
            pltpu.VMEM((SEQ_TILE // CHUNK, SUBLANES, d), F32),
            pltpu.VMEM((CHUNK, d), F32),
            pltpu.VMEM((2, CHUNK, d), F32)],
        compiler_params=pltpu.CompilerParams(
            dimension_semantics=("arbitrary", "arbitrary"), vmem_limit_bytes=VMEM_LIMIT),
        name="odd_mixer",
    )(h, *map(_operand, params))


def kernel(x, meta_tokens, mix_norm_g, mlp_norm_g, final_norm_g, ev_w_in, ev_conv_w, ev_conv_b, ev_ln_g,
           ev_ln_b, ev_pool_w, ev_pool_b, ev_pool_scale, ev_w_out, od_w_in, od_gnorm_g, od_w_out, lb_param,
           mlp_w1, mlp_w2):
    bn, seq, d = x.shape
    depth = mix_norm_g.shape[0]
    l_real = N_META + seq
    l_pad = -(-l_real // SEQ_TILE) * SEQ_TILE

    lb_all = jnp.cumsum(jax.nn.softmax(lb_param.astype(F32), axis=0), axis=0)
    lb_all = lb_all - lb_all[0]

    row = lambda v: v.reshape(1, -1).astype(F32)
    ev_w_in, ev_pool_w, ev_w_out, od_w_in, od_w_out, mlp_w1, mlp_w2 = (
        w.astype(BF16) for w in (ev_w_in, ev_pool_w, ev_w_out, od_w_in, od_w_out, mlp_w1, mlp_w2))
    h = x.astype(F32)
    for layer in range(depth):
        j = layer // 2
        if layer % 2 == 0:
            h = _even_mixer(h, row(mix_norm_g[layer]), _layer_of(ev_w_in, j), _layer_of(ev_conv_w, j),
                            row(ev_conv_b[j]), row(ev_ln_g[j]), row(ev_ln_b[j]), _layer_of(ev_pool_w, j),
                            row(ev_pool_b[j]), row(ev_pool_scale[j]), _layer_of(ev_w_out, j),
                            l_real=l_real, l_pad=l_pad, meta=meta_tokens.astype(F32) if layer == 0 else None)
        else:
            h = _odd_mixer(h, row(mix_norm_g[layer]), _layer_of(od_w_in, j), row(lb_all[layer]),
                           row(od_gnorm_g[j]), _layer_of(od_w_out, j), l_real=l_real)
        w1, w2 = _layer_of(mlp_w1, layer), _layer_of(mlp_w2, layer)
        if layer == depth - 1:
            return _mlp_final(h, row(mlp_norm_g[layer]), row(final_norm_g), w1, w2, seq=seq)
        h = _mlp(h.reshape(bn * l_pad, d), row(mlp_norm_g[layer]), w1, w2).reshape(bn, l_pad, d)
```

```python
import functools

import jax
import jax.numpy as jnp
from jax import lax
from jax.experimental import pallas as pl
from jax.experimental.pallas import tpu as pltpu

F32 = jnp.float32
BF16 = jnp.bfloat16

N_META = 16
CONV_WIDTH = 31
POOL_WINDOWS = (2, 4, 8, 16)
HGRN_HEAD_DIM = 128
EPS = 1e-6
LANES = 128
SUBLANES = 8

CHUNK = 64
MAX_EXPONENT = 80.0
SEQ_TILE = 832
ROW_CHUNK = 64
CONV_HALO = 32
POOL_HALO = 16
FF_CHUNK = 1024
VMEM_LIMIT = 56 * 1024 * 1024

assert max(POOL_WINDOWS) - 1 <= POOL_HALO and CONV_WIDTH - 1 <= CONV_HALO


def _sigmoid(x):
    return 0.5 * jnp.tanh(0.5 * x) + 0.5


def _rmsnorm(x, g):
    return x * lax.rsqrt(jnp.mean(x * x, axis=-1, keepdims=True) + EPS) * g


def _dot(a, b):
    return jnp.dot(a, b, preferred_element_type=F32)


def _dot_nt(a, b):
    return lax.dot_general(a, b, (((1,), (1,)), ((), ())), preferred_element_type=F32)


def _zero_pad_rows(y, first_row, l_real):
    row = first_row + lax.broadcasted_iota(jnp.int32, (y.shape[0], 1), 0)
    return jnp.where(row < l_real, y, 0.0)


def _resident(shape):
    nd = len(shape)
    return pl.BlockSpec(shape, lambda *_: (0,) * nd, pipeline_mode=pl.Buffered(1))


def _layer_of(stacked, layer):
    return stacked, layer


def _operand(p):
    return p[0] if isinstance(p, tuple) else p


def _param_spec(p):
    if not isinstance(p, tuple):
        return _resident(p.shape)
    stacked, layer = p
    nd = stacked.ndim
    return pl.BlockSpec((None,) + stacked.shape[1:], lambda *_: (layer,) + (0,) * (nd - 1),
                        pipeline_mode=pl.Buffered(1))


def _param_shape(p):
    return p[0].shape[1:] if isinstance(p, tuple) else p.shape


def _mlp_rows(x, g_ref, w1_ref, w2_ref):
    n = _rmsnorm(x, g_ref[...]).astype(BF16)
    acc = x
    for c in range(w1_ref.shape[1] // FF_CHUNK):
        cols = slice(c * FF_CHUNK, (c + 1) * FF_CHUNK)
        hid = _dot(n, w1_ref[:, cols])
        hid = jnp.square(jnp.maximum(hid, 0.0)).astype(BF16)
        acc = acc + _dot(hid, w2_ref[cols, :])
    return acc


def _mlp_kernel(h_ref, g_ref, w1_ref, w2_ref, o_ref):
    o_ref[...] = _mlp_rows(h_ref[...], g_ref, w1_ref, w2_ref)


def _mlp(h2d, g, w1, w2):
    rows, d = h2d.shape
    tile = pl.BlockSpec((SEQ_TILE, d), lambda i: (i, 0))
    return pl.pallas_call(
        _mlp_kernel,
        grid=(rows // SEQ_TILE,),
        in_specs=[tile, _resident(g.shape), _param_spec(w1), _param_spec(w2)],
        out_specs=tile,
        out_shape=jax.ShapeDtypeStruct(h2d.shape, F32),
        compiler_params=pltpu.CompilerParams(
            dimension_semantics=("arbitrary",), vmem_limit_bytes=VMEM_LIMIT),
        name="mlp",
    )(h2d, g, _operand(w1), _operand(w2))


def _mlp_final_kernel(h_ref, hnext_ref, g_ref, fg_ref, w1_ref, w2_ref, o_ref):
    x = jnp.concatenate([h_ref[N_META:, :], hnext_ref[0:N_META, :]], axis=0)
    o_ref[...] = _rmsnorm(_mlp_rows(x, g_ref, w1_ref, w2_ref), fg_ref[...])


def _mlp_final(h, g, fg, w1, w2, *, seq):
    b, lp, d = h.shape
    last = lp // SEQ_TILE - 1
    return pl.pallas_call(
        _mlp_final_kernel,
        grid=(b, pl.cdiv(seq, SEQ_TILE)),
        in_specs=[pl.BlockSpec((None, SEQ_TILE, d), lambda i, j: (i, j, 0)),
                  pl.BlockSpec((None, SEQ_TILE, d), lambda i, j: (i, jnp.minimum(j + 1, last), 0)),
                  _resident(g.shape), _resident(fg.shape), _param_spec(w1), _param_spec(w2)],
        out_specs=pl.BlockSpec((None, SEQ_TILE, d), lambda i, j: (i, j, 0)),
        out_shape=jax.ShapeDtypeStruct((b, seq, d), F32),
        compiler_params=pltpu.CompilerParams(
            dimension_semantics=("arbitrary", "arbitrary"), vmem_limit_bytes=VMEM_LIMIT),
        name="mlp_final",
    )(h, h, g, fg, _operand(w1), _operand(w2))


def _even_kernel(*refs, l_real, from_inputs):
    if from_inputs:
        meta_ref, xprev_ref, xcur_ref = refs[:3]
        refs = refs[2:]
    (h_ref, g_ref, win_ref, cw_ref, cb_ref, lng_ref, lnb_ref, pw_ref, pb_ref, ps_ref, wout_ref, o_ref,
     a_ext, p_ext, conv_ref, d_ref, y_ref) = refs
    tl = o_ref.shape[0]
    dc = cw_ref.shape[1]
    gd = pw_ref.shape[1]
    l = pl.program_id(1)

    if from_inputs:
        head = jnp.where(l == 0, meta_ref[...], xprev_ref[tl - N_META:, :])
        x = _zero_pad_rows(jnp.concatenate([head, xcur_ref[0:tl - N_META, :]], axis=0), l * tl, l_real)
    else:
        x = h_ref[...]
    n = _rmsnorm(x, g_ref[...]).astype(BF16)
    u = _dot(n, win_ref[...])

    @pl.when(l == 0)
    def _():
        a_ext[:, 0:CONV_HALO, :] = jnp.zeros((a_ext.shape[0], CONV_HALO, LANES), F32)
        p_ext[:, 0:POOL_HALO, :] = jnp.zeros((p_ext.shape[0], POOL_HALO, LANES), F32)

    @pl.when(l > 0)
    def _():
        a_ext[:, 0:CONV_HALO, :] = a_ext[:, tl:tl + CONV_HALO, :]
        p_ext[:, 0:POOL_HALO, :] = p_ext[:, tl:tl + POOL_HALO, :]

    a = u[:, 0:dc] * _sigmoid(u[:, dc:2 * dc])
    for lt in range(dc // LANES):
        a_ext[lt, CONV_HALO:CONV_HALO + tl, :] = a[:, lt * LANES:(lt + 1) * LANES]
    for gi in range(len(POOL_WINDOWS)):
        p_ext[gi, POOL_HALO:POOL_HALO + tl, :] = u[:, 2 * dc + gi * gd:2 * dc + (gi + 1) * gd]

    def chunk_rows(r):
        return pl.multiple_of(r * ROW_CHUNK, ROW_CHUNK)

    def accumulate(r):
        r0 = chunk_rows(r)
        for lt in range(dc // LANES):
            lanes = slice(lt * LANES, (lt + 1) * LANES)
            part = jnp.broadcast_to(cb_ref[:, lanes], (ROW_CHUNK, LANES))
            for j in range(CONV_WIDTH):
                off = CONV_HALO - (CONV_WIDTH - 1) + j
                part = part + cw_ref[j:j + 1, lanes] * a_ext[lt, pl.ds(r0 + off, ROW_CHUNK), :]
            conv_ref[pl.ds(r0, ROW_CHUNK), lanes] = part

        pos = l * tl + r0 + lax.broadcasted_iota(jnp.int32, (ROW_CHUNK, 1), 0)
        for gi, w in enumerate(POOL_WINDOWS):
            cur = p_ext[gi, pl.ds(r0 + POOL_HALO, ROW_CHUNK), :]
            s = cur
            for j in range(1, w):
                s = s + p_ext[gi, pl.ds(r0 + POOL_HALO - j, ROW_CHUNK), :]
            cnt = jnp.minimum(pos + 1, w).astype(F32)
            d_ref[pl.ds(r0, ROW_CHUNK), gi * gd:(gi + 1) * gd] = (s / cnt - cur).astype(BF16)

    def normalize(r):
        rows = pl.ds(chunk_rows(r), ROW_CHUNK)
        acc = conv_ref[rows, :]
        mu = jnp.mean(acc, axis=-1, keepdims=True)
        xc = acc - mu
        yn = xc * lax.rsqrt(jnp.mean(xc * xc, axis=-1, keepdims=True) + EPS) * lng_ref[...] + lnb_ref[...]
        y_ref[rows, 0:dc] = (yn * _sigmoid(yn)).astype(BF16)

    def chunk(r, carry):
        accumulate(r)
        normalize(r - 1)
        return carry

    n_row_chunks = tl // ROW_CHUNK
    accumulate(0)
    lax.fori_loop(1, n_row_chunks, chunk, 0)
    normalize(n_row_chunks - 1)

    for gi in range(len(POOL_WINDOWS)):
        lanes = slice(gi * gd, (gi + 1) * gd)
        yb = (_dot(d_ref[:, lanes], pw_ref[gi]) + pb_ref[:, lanes]) * ps_ref[:, lanes]
        y_ref[:, dc + gi * gd:dc + (gi + 1) * gd] = yb.astype(BF16)

    o_ref[...] = _zero_pad_rows(x + _dot(y_ref[...], wout_ref[...]), l * tl, l_real)


def _even_mixer(h, g, w_in, conv_w, conv_b, ln_g, ln_b, pool_w, pool_b, pool_scale, w_out, *, l_real, l_pad,
                meta=None):
    b, _, d = h.shape
    dc = _param_shape(conv_w)[1]
    dp = pool_scale.shape[1]
    assert dc % LANES == 0 and _param_shape(pool_w)[1:] == (LANES, LANES)
    tile = pl.BlockSpec((None, SEQ_TILE, d), lambda i, j: (i, j, 0))
    params = (g, w_in, conv_w, conv_b, ln_g, ln_b, pool_w, pool_b, pool_scale, w_out)
    if meta is None:
        stream_in, stream_specs = (h,), [tile]
    else:
        prev = pl.BlockSpec((None, SEQ_TILE, d), lambda i, j: (i, jnp.maximum(j - 1, 0), 0))
        stream_in, stream_specs = (meta, h, h), [_resident(meta.shape), prev, tile]
    return pl.pallas_call(
        functools.partial(_even_kernel, l_real=l_real, from_inputs=meta is not None),
        grid=(b, l_pad // SEQ_TILE),
        in_specs=stream_specs + [_param_spec(p) for p in params],
        out_specs=tile,
        out_shape=jax.ShapeDtypeStruct((b, l_pad, d), F32),
        scratch_shapes=[
            pltpu.VMEM((dc // LANES, CONV_HALO + SEQ_TILE, LANES), F32),
            pltpu.VMEM((len(POOL_WINDOWS), POOL_HALO + SEQ_TILE, LANES), F32),
            pltpu.VMEM((SEQ_TILE, dc), F32),
            pltpu.VMEM((SEQ_TILE, dp), BF16),
            pltpu.VMEM((SEQ_TILE, dc + dp), BF16),
        ],
        compiler_params=pltpu.CompilerParams(
            dimension_semantics=("arbitrary", "arbitrary"), vmem_limit_bytes=VMEM_LIMIT),
        name="even_mixer",
    )(*stream_in, *map(_operand, params))


def _chunk_cumsum(x, scan_ref):
    rows, n = x.shape
    halo = rows // 2
    parts = [x[:, lt * LANES:(lt + 1) * LANES] for lt in range(n // LANES)]
    shift = 1
    while shift < rows:
        for lt, part in enumerate(parts):
            scan_ref[lt, halo:halo + rows, :] = part
        parts = [part + scan_ref[lt, halo - shift:halo - shift + rows, :] for lt, part in enumerate(parts)]
        shift *= 2
    return jnp.concatenate(parts, axis=1)


def _odd_kernel(h_ref, g_ref, win_ref, lb_ref, gng_ref, wout_ref, o_ref,
                q_ref, f_ref, v_ref, gate_ref, qt_ref, qm_ref, km_ref, kd_ref, y_ref, st_ref, decay_ref,
                ostate_ref, opre_ref, scan_ref, *, l_real):
    tl, d = h_ref.shape
    hd = HGRN_HEAD_DIM
    heads = d // hd
    l = pl.program_id(1)

    @pl.when(l == 0)
    def _():
        st_ref[...] = jnp.zeros(st_ref.shape, F32)
        scan_ref[:, 0:CHUNK // 2, :] = jnp.zeros((scan_ref.shape[0], CHUNK // 2, LANES), F32)

    x = h_ref[...]
    n = _rmsnorm(x, g_ref[...]).astype(BF16)
    q = _dot(n, win_ref[:, 0:d])
    q_ref[...] = q * _sigmoid(q)
    f_ref[...] = _dot(n, win_ref[:, d:2 * d])
    v_ref[...] = _dot(n, win_ref[:, 2 * d:3 * d]).astype(BF16)
    gt = _dot(n, win_ref[:, 3 * d:4 * d])
    gate_ref[...] = (gt * _sigmoid(gt)).astype(BF16)

    causal = (lax.broadcasted_iota(jnp.int32, (CHUNK, CHUNK), 0)
              >= lax.broadcasted_iota(jnp.int32, (CHUNK, CHUNK), 1))
    eye = (lax.broadcasted_iota(jnp.int32, (hd, hd), 0)
           == lax.broadcasted_iota(jnp.int32, (hd, hd), 1)).astype(BF16)
    head_lanes = [slice(hh * hd, (hh + 1) * hd) for hh in range(heads)]
    n_chunks = tl // CHUNK

    def chunk_rows(c):
        return pl.ds(pl.multiple_of(c * CHUNK, CHUNK), CHUNK)

    def gates(c):
        lb = lb_ref[...]
        forget = lb + (1.0 - lb) * _sigmoid(f_ref[chunk_rows(c), :])
        return 1.0 - forget, _chunk_cumsum(jnp.log(forget), scan_ref)

    def normalize_and_gate(c):
        rows = chunk_rows(c)
        for ln in head_lanes:
            o = opre_ref[c % 2, :, ln]
            on = o * lax.rsqrt(jnp.mean(o * o, axis=-1, keepdims=True) + EPS) * gng_ref[...]
            y_ref[rows, ln] = (on * gate_ref[rows, ln].astype(F32)).astype(BF16)

    def prepare(c):
        rows = chunk_rows(c)
        k, b = gates(c)
        q = q_ref[rows, :]
        mid = b[CHUNK // 2 - 1:CHUNK // 2, :]
        b_last = b[CHUNK - 1:CHUNK, :]
        qt_ref[rows, :] = (q * jnp.exp(b)).astype(BF16)
        to_mid = jnp.exp(jnp.minimum(b - mid, MAX_EXPONENT))
        qm_ref[rows, :] = (q * to_mid).astype(BF16)
        km_ref[rows, :] = (k / to_mid).astype(BF16)
        kd_ref[rows, :] = (k * jnp.exp(b_last - b)).astype(BF16)
        decay_ref[c] = jnp.broadcast_to(jnp.exp(b_last), decay_ref.shape[1:])
        return jnp.max(jnp.maximum(-mid, mid - b_last))

    def attend(c):
        rows = chunk_rows(c)
        decay = decay_ref[c][0:1, :]
        scores = [jnp.where(causal, _dot_nt(qm_ref[rows, ln], km_ref[rows, ln]), 0.0).astype(BF16)
                  for ln in head_lanes]
        v_t = [_dot_nt(eye, v_ref[rows, ln]).astype(BF16) for ln in head_lanes]
        o_state = [_dot_nt(qt_ref[rows, ln], st_ref[hh].astype(BF16)) for hh, ln in enumerate(head_lanes)]
        o_intra = [_dot(scores[hh], v_ref[rows, ln]) for hh, ln in enumerate(head_lanes)]
        incr = [_dot(v_t[hh], kd_ref[rows, ln]) for hh, ln in enumerate(head_lanes)]
        for hh, ln in enumerate(head_lanes):
            ostate_ref[:, ln] = o_state[hh]
            st_ref[hh] = st_ref[hh] * decay[:, ln] + incr[hh]
            opre_ref[c % 2, :, ln] = o_intra[hh] + o_state[hh]

    def attend_exact(c):
        rows = chunk_rows(c)
        k, b = gates(c)
        q = q_ref[rows, :]
        v = v_ref[rows, :].astype(F32)
        t_idx = lax.broadcasted_iota(jnp.int32, (CHUNK, 1), 0)

        def source_row(s_idx, acc):
            pick = lambda a: jnp.sum(jnp.where(t_idx == s_idx, a, 0.0), axis=0, keepdims=True)
            b_s, k_s, v_s = pick(b), pick(k), pick(v)
            w = jnp.where(t_idx >= s_idx, q * jnp.exp(jnp.minimum(b - b_s, 0.0)) * k_s, 0.0)
            return acc + jnp.concatenate(
                [jnp.sum(w[:, ln], axis=-1, keepdims=True) * v_s[:, ln] for ln in head_lanes], axis=1)

        opre_ref[c % 2] = lax.fori_loop(0, CHUNK, source_row, ostate_ref[...])

    def step(c, behind):
        largest_exponent = prepare(c)
        if behind:
            normalize_and_gate(c - 1)
        attend(c)

        @pl.when(largest_exponent > MAX_EXPONENT)
        def _():
            attend_exact(c)

    def body(c, carry):
        step(c, True)
        return carry

    step(0, False)
    lax.fori_loop(1, n_chunks, body, 0)
    normalize_and_gate(n_chunks - 1)

    o_ref[...] = _zero_pad_rows(x + _dot(y_ref[...], wout_ref[...]), l * tl, l_real)


def _odd_mixer(h, g, w_in, lb, gn_g, w_out, *, l_real):
    b, lp, d = h.shape
    heads = d // HGRN_HEAD_DIM
    tile = pl.BlockSpec((None, SEQ_TILE, d), lambda i, j: (i, j, 0))
    params = (g, w_in, lb, gn_g, w_out)
    big = lambda dt: pltpu.VMEM((SEQ_TILE, d), dt)
    return pl.pallas_call(
        functools.partial(_odd_kernel, l_real=l_real),
        grid=(b, lp // SEQ_TILE),
        in_specs=[tile] + [_param_spec(p) for p in params],
        out_specs=tile,
        out_shape=jax.ShapeDtypeStruct(h.shape, F32),
        scratch_shapes=[big(F32), big(F32)] + [big(BF16)] * 7 + [
            pltpu.VMEM((heads, HGRN_HEAD_DIM, HGRN_HEAD_DIM), F32),
            pltpu.VMEM((SEQ_TILE // CHUNK, SUBLANES, d), F32),
            pltpu.VMEM((CHUNK, d), F32),
            pltpu.VMEM((2, CHUNK, d), F32),
            pltpu.VMEM((d // LANES, CHUNK // 2 + CHUNK, LANES), F32)],
        compiler_params=pltpu.CompilerParams(
            dimension_semantics=("arbitrary", "arbitrary"), vmem_limit_bytes=VMEM_LIMIT),
        name="odd_mixer",
    )(h, *map(_operand, params))


def kernel(x, meta_tokens, mix_norm_g, mlp_norm_g, final_norm_g, ev_w_in, ev_conv_w, ev_conv_b, ev_ln_g,
           ev_ln_b, ev_pool_w, ev_pool_b, ev_pool_scale, ev_w_out, od_w_in, od_gnorm_g, od_w_out, lb_param,
           mlp_w1, mlp_w2):
    bn, seq, d = x.shape
    depth = mix_norm_g.shape[0]
    l_real = N_META + seq
    l_pad = -(-l_real // SEQ_TILE) * SEQ_TILE

    lb_all = jnp.cumsum(jax.nn.softmax(lb_param.astype(F32), axis=0), axis=0)
    lb_all = lb_all - lb_all[0]

    row = lambda v: v.reshape(1, -1).astype(F32)
    ev_w_in, ev_pool_w, ev_w_out, od_w_in, od_w_out, mlp_w1, mlp_w2 = (
        w.astype(BF16) for w in (ev_w_in, ev_pool_w, ev_w_out, od_w_in, od_w_out, mlp_w1, mlp_w2))
    h = x.astype(F32)
    for layer in range(depth):
        j = layer // 2
        if layer % 2 == 0:
            h = _even_mixer(h, row(mix_norm_g[layer]), _layer_of(ev_w_in, j), _layer_of(ev_conv_w, j),
                            row(ev_conv_b[j]), row(ev_ln_g[j]), row(ev_ln_b[j]), _layer_of(ev_pool_w, j),
                            row(ev_pool_b[j]), row(ev_pool_scale[j]), _layer_of(ev_w_out, j),
                            l_real=l_real, l_pad=l_pad, meta=meta_tokens.astype(F32) if layer == 0 else None)
        else:
            h = _odd_mixer(h, row(mix_norm_g[layer]), _layer_of(od_w_in, j), row(lb_all[layer]),
                           row(od_gnorm_g[j]), _layer_of(od_w_out, j), l_real=l_real)
        w1, w2 = _layer_of(mlp_w1, layer), _layer_of(mlp_w2, layer)
        if layer == depth - 1:
            return _mlp_final(h, row(mlp_norm_g[layer]), row(final_norm_g), w1, w2, seq=seq)
        h = _mlp(h.reshape(bn * l_pad, d), row(mlp_norm_g[layer]), w1, w2).reshape(bn, l_pad, d)
```

```python
import functools

import jax
import jax.numpy as jnp
from jax import lax
from jax.experimental import pallas as pl
from jax.experimental.pallas import tpu as pltpu

F32 = jnp.float32
BF16 = jnp.bfloat16

N_META = 16
CONV_WIDTH = 31
POOL_WINDOWS = (2, 4, 8, 16)
HGRN_HEAD_DIM = 128
EPS = 1e-6
LANES = 128
SUBLANES = 8

CHUNK = 64
MAX_EXPONENT = 80.0
SEQ_TILE = 832
ROW_CHUNK = 64
CONV_HALO = 32
POOL_HALO = 16
FF_CHUNK = 1024
VMEM_LIMIT = 56 * 1024 * 1024

assert max(POOL_WINDOWS) - 1 <= POOL_HALO and CONV_WIDTH - 1 <= CONV_HALO


def _sigmoid(x):
    return 0.5 * jnp.tanh(0.5 * x) + 0.5


def _rmsnorm(x, g):
    return x * lax.rsqrt(jnp.mean(x * x, axis=-1, keepdims=True) + EPS) * g


def _dot(a, b):
    return jnp.dot(a, b, preferred_element_type=F32)


def _dot_nt(a, b):
    return lax.dot_general(a, b, (((1,), (1,)), ((), ())), preferred_element_type=F32)


def _zero_pad_rows(y, first_row, l_real):
    row = first_row + lax.broadcasted_iota(jnp.int32, (y.shape[0], 1), 0)
    return jnp.where(row < l_real, y, 0.0)


def _resident(shape):
    nd = len(shape)
    return pl.BlockSpec(shape, lambda *_: (0,) * nd, pipeline_mode=pl.Buffered(1))


def _layer_of(stacked, layer):
    return stacked, layer


def _operand(p):
    return p[0] if isinstance(p, tuple) else p


def _param_spec(p):
    if not isinstance(p, tuple):
        return _resident(p.shape)
    stacked, layer = p
    nd = stacked.ndim
    return pl.BlockSpec((None,) + stacked.shape[1:], lambda *_: (layer,) + (0,) * (nd - 1),
                        pipeline_mode=pl.Buffered(1))


def _param_shape(p):
    return p[0].shape[1:] if isinstance(p, tuple) else p.shape


def _mlp_rows(x, g_ref, w1_ref, w2_ref):
    n = _rmsnorm(x, g_ref[...]).astype(BF16)
    acc = x
    for c in range(w1_ref.shape[1] // FF_CHUNK):
        cols = slice(c * FF_CHUNK, (c + 1) * FF_CHUNK)
        hid = _dot(n, w1_ref[:, cols])
        hid = jnp.square(jnp.maximum(hid, 0.0)).astype(BF16)
        acc = acc + _dot(hid, w2_ref[cols, :])
    return acc


def _mlp_kernel(h_ref, g_ref, w1_ref, w2_ref, o_ref):
    o_ref[...] = _mlp_rows(h_ref[...], g_ref, w1_ref, w2_ref)


def _mlp(h2d, g, w1, w2):
    rows, d = h2d.shape
    tile = pl.BlockSpec((SEQ_TILE, d), lambda i: (i, 0))
    return pl.pallas_call(
        _mlp_kernel,
        grid=(rows // SEQ_TILE,),
        in_specs=[tile, _resident(g.shape), _param_spec(w1), _param_spec(w2)],
        out_specs=tile,
        out_shape=jax.ShapeDtypeStruct(h2d.shape, F32),
        compiler_params=pltpu.CompilerParams(
            dimension_semantics=("arbitrary",), vmem_limit_bytes=VMEM_LIMIT),
        name="mlp",
    )(h2d, g, _operand(w1), _operand(w2))


def _mlp_final_kernel(h_ref, hnext_ref, g_ref, fg_ref, w1_ref, w2_ref, o_ref):
    x = jnp.concatenate([h_ref[N_META:, :], hnext_ref[0:N_META, :]], axis=0)
    o_ref[...] = _rmsnorm(_mlp_rows(x, g_ref, w1_ref, w2_ref), fg_ref[...])


def _mlp_final(h, g, fg, w1, w2, *, seq):
    b, lp, d = h.shape
    last = lp // SEQ_TILE - 1
    return pl.pallas_call(
        _mlp_final_kernel,
        grid=(b, pl.cdiv(seq, SEQ_TILE)),
        in_specs=[pl.BlockSpec((None, SEQ_TILE, d), lambda i, j: (i, j, 0)),
                  pl.BlockSpec((None, SEQ_TILE, d), lambda i, j: (i, jnp.minimum(j + 1, last), 0)),
                  _resident(g.shape), _resident(fg.shape), _param_spec(w1), _param_spec(w2)],
        out_specs=pl.BlockSpec((None, SEQ_TILE, d), lambda i, j: (i, j, 0)),
        out_shape=jax.ShapeDtypeStruct((b, seq, d), F32),
        compiler_params=pltpu.CompilerParams(
            dimension_semantics=("arbitrary", "arbitrary"), vmem_limit_bytes=VMEM_LIMIT),
        name="mlp_final",
    )(h, h, g, fg, _operand(w1), _operand(w2))


def _even_kernel(*refs, l_real, from_inputs):
    if from_inputs:
        meta_ref, xprev_ref, xcur_ref = refs[:3]
        refs = refs[2:]
    (h_ref, g_ref, win_ref, cw_ref, cb_ref, lng_ref, lnb_ref, pw_ref, pb_ref, ps_ref, wout_ref, o_ref,
     a_ext, p_ext, conv_ref, d_ref, y_ref) = refs
    tl = o_ref.shape[0]
    dc = cw_ref.shape[1]
    gd = pw_ref.shape[1]
    l = pl.program_id(1)

    if from_inputs:
        head = jnp.where(l == 0, meta_ref[...], xprev_ref[tl - N_META:, :])
        x = _zero_pad_rows(jnp.concatenate([head, xcur_ref[0:tl - N_META, :]], axis=0), l * tl, l_real)
    else:
        x = h_ref[...]
    n = _rmsnorm(x, g_ref[...]).astype(BF16)
    u = _dot(n, win_ref[...])

    @pl.when(l == 0)
    def _():
        a_ext[:, 0:CONV_HALO, :] = jnp.zeros((a_ext.shape[0], CONV_HALO, LANES), F32)
        p_ext[:, 0:POOL_HALO, :] = jnp.zeros((p_ext.shape[0], POOL_HALO, LANES), F32)

    @pl.when(l > 0)
    def _():
        a_ext[:, 0:CONV_HALO, :] = a_ext[:, tl:tl + CONV_HALO, :]
        p_ext[:, 0:POOL_HALO, :] = p_ext[:, tl:tl + POOL_HALO, :]

    a = u[:, 0:dc] * _sigmoid(u[:, dc:2 * dc])
    for lt in range(dc // LANES):
        a_ext[lt, CONV_HALO:CONV_HALO + tl, :] = a[:, lt * LANES:(lt + 1) * LANES]
    for gi in range(len(POOL_WINDOWS)):
        p_ext[gi, POOL_HALO:POOL_HALO + tl, :] = u[:, 2 * dc + gi * gd:2 * dc + (gi + 1) * gd]

    def chunk_rows(r):
        return pl.multiple_of(r * ROW_CHUNK, ROW_CHUNK)

    def accumulate(r):
        r0 = chunk_rows(r)
        for lt in range(dc // LANES):
            lanes = slice(lt * LANES, (lt + 1) * LANES)
            part = jnp.broadcast_to(cb_ref[:, lanes], (ROW_CHUNK, LANES))
            for j in range(CONV_WIDTH):
                off = CONV_HALO - (CONV_WIDTH - 1) + j
                part = part + cw_ref[j:j + 1, lanes] * a_ext[lt, pl.ds(r0 + off, ROW_CHUNK), :]
            conv_ref[pl.ds(r0, ROW_CHUNK), lanes] = part

        pos = l * tl + r0 + lax.broadcasted_iota(jnp.int32, (ROW_CHUNK, 1), 0)
        for gi, w in enumerate(POOL_WINDOWS):
            cur = p_ext[gi, pl.ds(r0 + POOL_HALO, ROW_CHUNK), :]
            s = cur
            for j in range(1, w):
                s = s + p_ext[gi, pl.ds(r0 + POOL_HALO - j, ROW_CHUNK), :]
            cnt = jnp.minimum(pos + 1, w).astype(F32)
            d_ref[pl.ds(r0, ROW_CHUNK), gi * gd:(gi + 1) * gd] = (s / cnt - cur).astype(BF16)

    def normalize(r):
        rows = pl.ds(chunk_rows(r), ROW_CHUNK)
        acc = conv_ref[rows, :]
        mu = jnp.mean(acc, axis=-1, keepdims=True)
        xc = acc - mu
        yn = xc * lax.rsqrt(jnp.mean(xc * xc, axis=-1, keepdims=True) + EPS) * lng_ref[...] + lnb_ref[...]
        y_ref[rows, 0:dc] = (yn * _sigmoid(yn)).astype(BF16)

    def chunk(r, carry):
        accumulate(r)
        normalize(r - 1)
        return carry

    n_row_chunks = tl // ROW_CHUNK
    accumulate(0)
    lax.fori_loop(1, n_row_chunks, chunk, 0)
    normalize(n_row_chunks - 1)

    for gi in range(len(POOL_WINDOWS)):
        lanes = slice(gi * gd, (gi + 1) * gd)
        yb = (_dot(d_ref[:, lanes], pw_ref[gi]) + pb_ref[:, lanes]) * ps_ref[:, lanes]
        y_ref[:, dc + gi * gd:dc + (gi + 1) * gd] = yb.astype(BF16)

    o_ref[...] = _zero_pad_rows(x + _dot(y_ref[...], wout_ref[...]), l * tl, l_real)


def _even_mixer(h, g, w_in, conv_w, conv_b, ln_g, ln_b, pool_w, pool_b, pool_scale, w_out, *, l_real, l_pad,
                meta=None):
    b, _, d = h.shape
    dc = _param_shape(conv_w)[1]
    dp = pool_scale.shape[1]
    assert dc % LANES == 0 and _param_shape(pool_w)[1:] == (LANES, LANES)
    tile = pl.BlockSpec((None, SEQ_TILE, d), lambda i, j: (i, j, 0))
    params = (g, w_in, conv_w, conv_b, ln_g, ln_b, pool_w, pool_b, pool_scale, w_out)
    if meta is None:
        stream_in, stream_specs = (h,), [tile]
    else:
        prev = pl.BlockSpec((None, SEQ_TILE, d), lambda i, j: (i, jnp.maximum(j - 1, 0), 0))
        stream_in, stream_specs = (meta, h, h), [_resident(meta.shape), prev, tile]
    return pl.pallas_call(
        functools.partial(_even_kernel, l_real=l_real, from_inputs=meta is not None),
        grid=(b, l_pad // SEQ_TILE),
        in_specs=stream_specs + [_param_spec(p) for p in params],
        out_specs=tile,
        out_shape=jax.ShapeDtypeStruct((b, l_pad, d), F32),
        scratch_shapes=[
            pltpu.VMEM((dc // LANES, CONV_HALO + SEQ_TILE, LANES), F32),
            pltpu.VMEM((len(POOL_WINDOWS), POOL_HALO + SEQ_TILE, LANES), F32),
            pltpu.VMEM((SEQ_TILE, dc), F32),
            pltpu.VMEM((SEQ_TILE, dp), BF16),
            pltpu.VMEM((SEQ_TILE, dc + dp), BF16),
        ],
        compiler_params=pltpu.CompilerParams(
            dimension_semantics=("arbitrary", "arbitrary"), vmem_limit_bytes=VMEM_LIMIT),
        name="even_mixer",
    )(*stream_in, *map(_operand, params))


def _chunk_cumsum(x, scan_ref):
    rows, n = x.shape
    halo = rows // 2
    parts = [x[:, lt * LANES:(lt + 1) * LANES] for lt in range(n // LANES)]
    shift = 1
    while shift < rows:
        for lt, part in enumerate(parts):
            scan_ref[lt, halo:halo + rows, :] = part
        parts = [part + scan_ref[lt, halo - shift:halo - shift + rows, :] for lt, part in enumerate(parts)]
        shift *= 2
    return jnp.concatenate(parts, axis=1)


def _odd_kernel(h_ref, g_ref, win_ref, lb_ref, gng_ref, wout_ref, o_ref,
                q_ref, logf_ref, k_ref, v_ref, gate_ref, qt_ref, qm_ref, km_ref, kd_ref, y_ref, st_ref, decay_ref,
                ostate_ref, opre_ref, scan_ref, *, l_real):
    tl, d = h_ref.shape
    hd = HGRN_HEAD_DIM
    heads = d // hd
    l = pl.program_id(1)

    @pl.when(l == 0)
    def _():
        st_ref[...] = jnp.zeros(st_ref.shape, F32)
        scan_ref[:, 0:CHUNK // 2, :] = jnp.zeros((scan_ref.shape[0], CHUNK // 2, LANES), F32)

    x = h_ref[...]
    n = _rmsnorm(x, g_ref[...]).astype(BF16)
    q = _dot(n, win_ref[:, 0:d])
    q_ref[...] = q * _sigmoid(q)
    lb = lb_ref[...]
    forget = lb + (1.0 - lb) * _sigmoid(_dot(n, win_ref[:, d:2 * d]))
    logf_ref[...] = jnp.log(forget)
    k_ref[...] = 1.0 - forget
    v_ref[...] = _dot(n, win_ref[:, 2 * d:3 * d]).astype(BF16)
    gt = _dot(n, win_ref[:, 3 * d:4 * d])
    gate_ref[...] = (gt * _sigmoid(gt)).astype(BF16)

    causal = (lax.broadcasted_iota(jnp.int32, (CHUNK, CHUNK), 0)
              >= lax.broadcasted_iota(jnp.int32, (CHUNK, CHUNK), 1))
    eye = (lax.broadcasted_iota(jnp.int32, (hd, hd), 0)
           == lax.broadcasted_iota(jnp.int32, (hd, hd), 1)).astype(BF16)
    head_lanes = [slice(hh * hd, (hh + 1) * hd) for hh in range(heads)]
    n_chunks = tl // CHUNK

    def chunk_rows(c):
        return pl.ds(pl.multiple_of(c * CHUNK, CHUNK), CHUNK)

    def gates(c):
        rows = chunk_rows(c)
        return k_ref[rows, :], _chunk_cumsum(logf_ref[rows, :], scan_ref)

    def normalize_and_gate(c):
        rows = chunk_rows(c)
        for ln in head_lanes:
            o = opre_ref[c % 2, :, ln]
            on = o * lax.rsqrt(jnp.mean(o * o, axis=-1, keepdims=True) + EPS) * gng_ref[...]
            y_ref[rows, ln] = (on * gate_ref[rows, ln].astype(F32)).astype(BF16)

    def prepare(c):
        rows = chunk_rows(c)
        k, b = gates(c)
        q = q_ref[rows, :]
        mid = b[CHUNK // 2 - 1:CHUNK // 2, :]
        b_last = b[CHUNK - 1:CHUNK, :]
        qt_ref[rows, :] = (q * jnp.exp(b)).astype(BF16)
        to_mid = jnp.exp(jnp.minimum(b - mid, MAX_EXPONENT))
        qm_ref[rows, :] = (q * to_mid).astype(BF16)
        km_ref[rows, :] = (k / to_mid).astype(BF16)
        kd_ref[rows, :] = (k * jnp.exp(b_last - b)).astype(BF16)
        decay_ref[c] = jnp.broadcast_to(jnp.exp(b_last), decay_ref.shape[1:])
        return jnp.max(jnp.maximum(-mid, mid - b_last))

    def attend(c):
        rows = chunk_rows(c)
        decay = decay_ref[c][0:1, :]
        scores = [jnp.where(causal, _dot_nt(qm_ref[rows, ln], km_ref[rows, ln]), 0.0).astype(BF16)
                  for ln in head_lanes]
        v_t = [_dot_nt(eye, v_ref[rows, ln]).astype(BF16) for ln in head_lanes]
        o_state = [_dot_nt(qt_ref[rows, ln], st_ref[hh].astype(BF16)) for hh, ln in enumerate(head_lanes)]
        o_intra = [_dot(scores[hh], v_ref[rows, ln]) for hh, ln in enumerate(head_lanes)]
        incr = [_dot(v_t[hh], kd_ref[rows, ln]) for hh, ln in enumerate(head_lanes)]
        for hh, ln in enumerate(head_lanes):
            ostate_ref[:, ln] = o_state[hh]
            st_ref[hh] = st_ref[hh] * decay[:, ln] + incr[hh]
            opre_ref[c % 2, :, ln] = o_intra[hh] + o_state[hh]

    def attend_exact(c):
        rows = chunk_rows(c)
        k, b = gates(c)
        q = q_ref[rows, :]
        v = v_ref[rows, :].astype(F32)
        t_idx = lax.broadcasted_iota(jnp.int32, (CHUNK, 1), 0)

        def source_row(s_idx, acc):
            pick = lambda a: jnp.sum(jnp.where(t_idx == s_idx, a, 0.0), axis=0, keepdims=True)
            b_s, k_s, v_s = pick(b), pick(k), pick(v)
            w = jnp.where(t_idx >= s_idx, q * jnp.exp(jnp.minimum(b - b_s, 0.0)) * k_s, 0.0)
            return acc + jnp.concatenate(
                [jnp.sum(w[:, ln], axis=-1, keepdims=True) * v_s[:, ln] for ln in head_lanes], axis=1)

        opre_ref[c % 2] = lax.fori_loop(0, CHUNK, source_row, ostate_ref[...])

    def step(c, behind):
        largest_exponent = prepare(c)
        if behind:
            normalize_and_gate(c - 1)
        attend(c)

        @pl.when(largest_exponent > MAX_EXPONENT)
        def _():
            attend_exact(c)

    def body(c, carry):
        step(c, True)
        return carry

    step(0, False)
    lax.fori_loop(1, n_chunks, body, 0)
    normalize_and_gate(n_chunks - 1)

    o_ref[...] = _zero_pad_rows(x + _dot(y_ref[...], wout_ref[...]), l * tl, l_real)


def _odd_mixer(h, g, w_in, lb, gn_g, w_out, *, l_real):
    b, lp, d = h.shape
    heads = d // HGRN_HEAD_DIM
    tile = pl.BlockSpec((None, SEQ_TILE, d), lambda i, j: (i, j, 0))
    params = (g, w_in, lb, gn_g, w_out)
    big = lambda dt: pltpu.VMEM((SEQ_TILE, d), dt)
    return pl.pallas_call(
        functools.partial(_odd_kernel, l_real=l_real),
        grid=(b, lp // SEQ_TILE),
        in_specs=[tile] + [_param_spec(p) for p in params],
        out_specs=tile,
        out_shape=jax.ShapeDtypeStruct(h.shape, F32),
        scratch_shapes=[big(F32)] * 3 + [big(BF16)] * 7 + [
            pltpu.VMEM((heads, HGRN_HEAD_DIM, HGRN_HEAD_DIM), F32),
            pltpu.VMEM((SEQ_TILE // CHUNK, SUBLANES, d), F32),
            pltpu.VMEM((CHUNK, d), F32),
            pltpu.VMEM((2, CHUNK, d), F32),
            pltpu.VMEM((d // LANES, CHUNK // 2 + CHUNK, LANES), F32)],
        compiler_params=pltpu.CompilerParams(
            dimension_semantics=("arbitrary", "arbitrary"), vmem_limit_bytes=VMEM_LIMIT),
        name="odd_mixer",
    )(h, *map(_operand, params))


def kernel(x, meta_tokens, mix_norm_g, mlp_norm_g, final_norm_g, ev_w_in, ev_conv_w, ev_conv_b, ev_ln_g,
           ev_ln_b, ev_pool_w, ev_pool_b, ev_pool_scale, ev_w_out, od_w_in, od_gnorm_g, od_w_out, lb_param,
           mlp_w1, mlp_w2):
    bn, seq, d = x.shape
    depth = mix_norm_g.shape[0]
    l_real = N_META + seq
    l_pad = -(-l_real // SEQ_TILE) * SEQ_TILE

    lb_all = jnp.cumsum(jax.nn.softmax(lb_param.astype(F32), axis=0), axis=0)
    lb_all = lb_all - lb_all[0]

    row = lambda v: v.reshape(1, -1).astype(F32)
    ev_w_in, ev_pool_w, ev_w_out, od_w_in, od_w_out, mlp_w1, mlp_w2 = (
        w.astype(BF16) for w in (ev_w_in, ev_pool_w, ev_w_out, od_w_in, od_w_out, mlp_w1, mlp_w2))
    h = x.astype(F32)
    for layer in range(depth):
        j = layer // 2
        if layer % 2 == 0:
            h = _even_mixer(h, row(mix_norm_g[layer]), _layer_of(ev_w_in, j), _layer_of(ev_conv_w, j),
                            row(ev_conv_b[j]), row(ev_ln_g[j]), row(ev_ln_b[j]), _layer_of(ev_pool_w, j),
                            row(ev_pool_b[j]), row(ev_pool_scale[j]), _layer_of(ev_w_out, j),
                            l_real=l_real, l_pad=l_pad, meta=meta_tokens.astype(F32) if layer == 0 else None)
        else:
            h = _odd_mixer(h, row(mix_norm_g[layer]), _layer_of(od_w_in, j), row(lb_all[layer]),
                           row(od_gnorm_g[j]), _layer_of(od_w_out, j), l_real=l_real)
        w1, w2 = _layer_of(mlp_w1, layer), _layer_of(mlp_w2, layer)
        if layer == depth - 1:
            return _mlp_final(h, row(mlp_norm_g[layer]), row(final_norm_g), w1, w2, seq=seq)
        h = _mlp(h.reshape(bn * l_pad, d), row(mlp_norm_g[layer]), w1, w2).reshape(bn, l_pad, d)
```

```python
import functools

import jax
import jax.numpy as jnp
from jax import lax
from jax.experimental import pallas as pl
from jax.experimental.pallas import tpu as pltpu

F32 = jnp.float32
BF16 = jnp.bfloat16

N_META = 16
CONV_WIDTH = 31
POOL_WINDOWS = (2, 4, 8, 16)
HGRN_HEAD_DIM = 128
EPS = 1e-6
LANES = 128
SUBLANES = 8

CHUNK = 64
MAX_EXPONENT = 80.0
SEQ_TILE = 832
ROW_CHUNK = 64
CONV_HALO = 32
POOL_HALO = 16
FF_CHUNK = 1024
VMEM_LIMIT = 56 * 1024 * 1024

assert max(POOL_WINDOWS) - 1 <= POOL_HALO and CONV_WIDTH - 1 <= CONV_HALO


def _sigmoid(x):
    return 0.5 * jnp.tanh(0.5 * x) + 0.5


def _sigmoid_tail_exact(x):
    return 1.0 / (1.0 + jnp.exp(-x))


def _rmsnorm(x, g):
    return x * lax.rsqrt(jnp.mean(x * x, axis=-1, keepdims=True) + EPS) * g


def _dot(a, b):
    return jnp.dot(a, b, preferred_element_type=F32)


def _dot_nt(a, b):
    return lax.dot_general(a, b, (((1,), (1,)), ((), ())), preferred_element_type=F32)


def _zero_pad_rows(y, first_row, l_real):
    row = first_row + lax.broadcasted_iota(jnp.int32, (y.shape[0], 1), 0)
    return jnp.where(row < l_real, y, 0.0)


def _resident(shape):
    nd = len(shape)
    return pl.BlockSpec(shape, lambda *_: (0,) * nd, pipeline_mode=pl.Buffered(1))


def _layer_of(stacked, layer):
    return stacked, layer


def _operand(p):
    return p[0] if isinstance(p, tuple) else p


def _param_spec(p):
    if not isinstance(p, tuple):
        return _resident(p.shape)
    stacked, layer = p
    nd = stacked.ndim
    return pl.BlockSpec((None,) + stacked.shape[1:], lambda *_: (layer,) + (0,) * (nd - 1),
                        pipeline_mode=pl.Buffered(1))


def _param_shape(p):
    return p[0].shape[1:] if isinstance(p, tuple) else p.shape


def _mlp_rows(x, g_ref, w1_ref, w2_ref):
    n = _rmsnorm(x, g_ref[...]).astype(BF16)
    acc = x
    for c in range(w1_ref.shape[1] // FF_CHUNK):
        cols = slice(c * FF_CHUNK, (c + 1) * FF_CHUNK)
        hid = _dot(n, w1_ref[:, cols])
        hid = jnp.square(jnp.maximum(hid, 0.0)).astype(BF16)
        acc = acc + _dot(hid, w2_ref[cols, :])
    return acc


def _mlp_kernel(h_ref, g_ref, w1_ref, w2_ref, o_ref):
    o_ref[...] = _mlp_rows(h_ref[...], g_ref, w1_ref, w2_ref)


def _mlp(h2d, g, w1, w2):
    rows, d = h2d.shape
    tile = pl.BlockSpec((SEQ_TILE, d), lambda i: (i, 0))
    return pl.pallas_call(
        _mlp_kernel,
        grid=(rows // SEQ_TILE,),
        in_specs=[tile, _resident(g.shape), _param_spec(w1), _param_spec(w2)],
        out_specs=tile,
        out_shape=jax.ShapeDtypeStruct(h2d.shape, F32),
        compiler_params=pltpu.CompilerParams(
            dimension_semantics=("arbitrary",), vmem_limit_bytes=VMEM_LIMIT),
        name="mlp",
    )(h2d, g, _operand(w1), _operand(w2))


def _mlp_final_kernel(h_ref, hnext_ref, g_ref, fg_ref, w1_ref, w2_ref, o_ref):
    x = jnp.concatenate([h_ref[N_META:, :], hnext_ref[0:N_META, :]], axis=0)
    o_ref[...] = _rmsnorm(_mlp_rows(x, g_ref, w1_ref, w2_ref), fg_ref[...])


def _mlp_final(h, g, fg, w1, w2, *, seq):
    b, lp, d = h.shape
    last = lp // SEQ_TILE - 1
    return pl.pallas_call(
        _mlp_final_kernel,
        grid=(b, pl.cdiv(seq, SEQ_TILE)),
        in_specs=[pl.BlockSpec((None, SEQ_TILE, d), lambda i, j: (i, j, 0)),
                  pl.BlockSpec((None, SEQ_TILE, d), lambda i, j: (i, jnp.minimum(j + 1, last), 0)),
                  _resident(g.shape), _resident(fg.shape), _param_spec(w1), _param_spec(w2)],
        out_specs=pl.BlockSpec((None, SEQ_TILE, d), lambda i, j: (i, j, 0)),
        out_shape=jax.ShapeDtypeStruct((b, seq, d), F32),
        compiler_params=pltpu.CompilerParams(
            dimension_semantics=("arbitrary", "arbitrary"), vmem_limit_bytes=VMEM_LIMIT),
        name="mlp_final",
    )(h, h, g, fg, _operand(w1), _operand(w2))


def _even_kernel(*refs, l_real, from_inputs):
    if from_inputs:
        meta_ref, xprev_ref, xcur_ref = refs[:3]
        refs = refs[2:]
    (h_ref, g_ref, win_ref, cw_ref, cb_ref, lng_ref, lnb_ref, pw_ref, pb_ref, ps_ref, wout_ref, o_ref,
     a_ext, p_ext, conv_ref, d_ref, y_ref) = refs
    tl = o_ref.shape[0]
    dc = cw_ref.shape[1]
    gd = pw_ref.shape[1]
    l = pl.program_id(1)

    if from_inputs:
        head = jnp.where(l == 0, meta_ref[...], xprev_ref[tl - N_META:, :])
        x = _zero_pad_rows(jnp.concatenate([head, xcur_ref[0:tl - N_META, :]], axis=0), l * tl, l_real)
    else:
        x = h_ref[...]
    n = _rmsnorm(x, g_ref[...]).astype(BF16)
    u = _dot(n, win_ref[...])

    @pl.when(l == 0)
    def _():
        a_ext[:, 0:CONV_HALO, :] = jnp.zeros((a_ext.shape[0], CONV_HALO, LANES), F32)
        p_ext[:, 0:POOL_HALO, :] = jnp.zeros((p_ext.shape[0], POOL_HALO, LANES), F32)

    @pl.when(l > 0)
    def _():
        a_ext[:, 0:CONV_HALO, :] = a_ext[:, tl:tl + CONV_HALO, :]
        p_ext[:, 0:POOL_HALO, :] = p_ext[:, tl:tl + POOL_HALO, :]

    a = u[:, 0:dc] * _sigmoid(u[:, dc:2 * dc])
    for lt in range(dc // LANES):
        a_ext[lt, CONV_HALO:CONV_HALO + tl, :] = a[:, lt * LANES:(lt + 1) * LANES]
    for gi in range(len(POOL_WINDOWS)):
        p_ext[gi, POOL_HALO:POOL_HALO + tl, :] = u[:, 2 * dc + gi * gd:2 * dc + (gi + 1) * gd]

    def chunk_rows(r):
        return pl.multiple_of(r * ROW_CHUNK, ROW_CHUNK)

    def accumulate(r):
        r0 = chunk_rows(r)
        for lt in range(dc // LANES):
            lanes = slice(lt * LANES, (lt + 1) * LANES)
            part = jnp.broadcast_to(cb_ref[:, lanes], (ROW_CHUNK, LANES))
            for j in range(CONV_WIDTH):
                off = CONV_HALO - (CONV_WIDTH - 1) + j
                part = part + cw_ref[j:j + 1, lanes] * a_ext[lt, pl.ds(r0 + off, ROW_CHUNK), :]
            conv_ref[pl.ds(r0, ROW_CHUNK), lanes] = part

        pos = l * tl + r0 + lax.broadcasted_iota(jnp.int32, (ROW_CHUNK, 1), 0)
        for gi, w in enumerate(POOL_WINDOWS):
            cur = p_ext[gi, pl.ds(r0 + POOL_HALO, ROW_CHUNK), :]
            s = cur
            for j in range(1, w):
                s = s + p_ext[gi, pl.ds(r0 + POOL_HALO - j, ROW_CHUNK), :]
            cnt = jnp.minimum(pos + 1, w).astype(F32)
            d_ref[pl.ds(r0, ROW_CHUNK), gi * gd:(gi + 1) * gd] = (s / cnt - cur).astype(BF16)

    def normalize(r):
        rows = pl.ds(chunk_rows(r), ROW_CHUNK)
        acc = conv_ref[rows, :]
        mu = jnp.mean(acc, axis=-1, keepdims=True)
        xc = acc - mu
        yn = xc * lax.rsqrt(jnp.mean(xc * xc, axis=-1, keepdims=True) + EPS) * lng_ref[...] + lnb_ref[...]
        y_ref[rows, 0:dc] = (yn * _sigmoid(yn)).astype(BF16)

    def chunk(r, carry):
        accumulate(r)
        normalize(r - 1)
        return carry

    n_row_chunks = tl // ROW_CHUNK
    accumulate(0)
    lax.fori_loop(1, n_row_chunks, chunk, 0)
    normalize(n_row_chunks - 1)

    for gi in range(len(POOL_WINDOWS)):
        lanes = slice(gi * gd, (gi + 1) * gd)
        yb = (_dot(d_ref[:, lanes], pw_ref[gi]) + pb_ref[:, lanes]) * ps_ref[:, lanes]
        y_ref[:, dc + gi * gd:dc + (gi + 1) * gd] = yb.astype(BF16)

    o_ref[...] = _zero_pad_rows(x + _dot(y_ref[...], wout_ref[...]), l * tl, l_real)


def _even_mixer(h, g, w_in, conv_w, conv_b, ln_g, ln_b, pool_w, pool_b, pool_scale, w_out, *, l_real, l_pad,
                meta=None):
    b, _, d = h.shape
    dc = _param_shape(conv_w)[1]
    dp = pool_scale.shape[1]
    assert dc % LANES == 0 and _param_shape(pool_w)[1:] == (LANES, LANES)
    tile = pl.BlockSpec((None, SEQ_TILE, d), lambda i, j: (i, j, 0))
    params = (g, w_in, conv_w, conv_b, ln_g, ln_b, pool_w, pool_b, pool_scale, w_out)
    if meta is None:
        stream_in, stream_specs = (h,), [tile]
    else:
        prev = pl.BlockSpec((None, SEQ_TILE, d), lambda i, j: (i, jnp.maximum(j - 1, 0), 0))
        stream_in, stream_specs = (meta, h, h), [_resident(meta.shape), prev, tile]
    return pl.pallas_call(
        functools.partial(_even_kernel, l_real=l_real, from_inputs=meta is not None),
        grid=(b, l_pad // SEQ_TILE),
        in_specs=stream_specs + [_param_spec(p) for p in params],
        out_specs=tile,
        out_shape=jax.ShapeDtypeStruct((b, l_pad, d), F32),
        scratch_shapes=[
            pltpu.VMEM((dc // LANES, CONV_HALO + SEQ_TILE, LANES), F32),
            pltpu.VMEM((len(POOL_WINDOWS), POOL_HALO + SEQ_TILE, LANES), F32),
            pltpu.VMEM((SEQ_TILE, dc), F32),
            pltpu.VMEM((SEQ_TILE, dp), BF16),
            pltpu.VMEM((SEQ_TILE, dc + dp), BF16),
        ],
        compiler_params=pltpu.CompilerParams(
            dimension_semantics=("arbitrary", "arbitrary"), vmem_limit_bytes=VMEM_LIMIT),
        name="even_mixer",
    )(*stream_in, *map(_operand, params))


def _chunk_cumsum(x, scan_ref):
    rows, n = x.shape
    halo = rows // 2
    parts = [x[:, lt * LANES:(lt + 1) * LANES] for lt in range(n // LANES)]
    shift = 1
    while shift < rows:
        if shift % SUBLANES:
            for lt, part in enumerate(parts):
                scan_ref[lt, halo:halo + rows, :] = part
            parts = [part + scan_ref[lt, halo - shift:halo - shift + rows, :] for lt, part in enumerate(parts)]
        else:
            pad = jnp.zeros((shift, LANES), F32)
            parts = [part + jnp.concatenate([pad, part[:rows - shift, :]], axis=0) for part in parts]
        shift *= 2
    return jnp.concatenate(parts, axis=1)


def _odd_kernel(h_ref, g_ref, win_ref, lb_ref, gng_ref, wout_ref, o_ref,
                q_ref, logf_ref, k_ref, v_ref, gate_ref, qt_ref, qm_ref, km_ref, kd_ref, y_ref, st_ref, decay_ref,
                ostate_ref, opre_ref, scan_ref, *, l_real):
    tl, d = h_ref.shape
    hd = HGRN_HEAD_DIM
    heads = d // hd
    l = pl.program_id(1)

    @pl.when(l == 0)
    def _():
        st_ref[...] = jnp.zeros(st_ref.shape, F32)
        scan_ref[:, 0:CHUNK // 2, :] = jnp.zeros((scan_ref.shape[0], CHUNK // 2, LANES), F32)

    x = h_ref[...]
    n = _rmsnorm(x, g_ref[...]).astype(BF16)
    q = _dot(n, win_ref[:, 0:d])
    q_ref[...] = q * _sigmoid(q)
    lb = lb_ref[...]
    forget = lb + (1.0 - lb) * _sigmoid_tail_exact(_dot(n, win_ref[:, d:2 * d]))
    logf_ref[...] = jnp.log(forget)
    k_ref[...] = 1.0 - forget
    v_ref[...] = _dot(n, win_ref[:, 2 * d:3 * d]).astype(BF16)
    gt = _dot(n, win_ref[:, 3 * d:4 * d])
    gate_ref[...] = (gt * _sigmoid(gt)).astype(BF16)

    causal = (lax.broadcasted_iota(jnp.int32, (CHUNK, CHUNK), 0)
              >= lax.broadcasted_iota(jnp.int32, (CHUNK, CHUNK), 1))
    eye = (lax.broadcasted_iota(jnp.int32, (hd, hd), 0)
           == lax.broadcasted_iota(jnp.int32, (hd, hd), 1)).astype(BF16)
    head_lanes = [slice(hh * hd, (hh + 1) * hd) for hh in range(heads)]
    n_chunks = tl // CHUNK

    def chunk_rows(c):
        return pl.ds(pl.multiple_of(c * CHUNK, CHUNK), CHUNK)

    def gates(c):
        rows = chunk_rows(c)
        return k_ref[rows, :], _chunk_cumsum(logf_ref[rows, :], scan_ref)

    def normalize_and_gate(c):
        rows = chunk_rows(c)
        for ln in head_lanes:
            o = opre_ref[c % 2, :, ln]
            on = o * lax.rsqrt(jnp.mean(o * o, axis=-1, keepdims=True) + EPS) * gng_ref[...]
            y_ref[rows, ln] = (on * gate_ref[rows, ln].astype(F32)).astype(BF16)

    def prepare(c):
        rows = chunk_rows(c)
        k, b = gates(c)
        q = q_ref[rows, :]
        mid = b[CHUNK // 2 - 1:CHUNK // 2, :]
        b_last = b[CHUNK - 1:CHUNK, :]
        qt_ref[rows, :] = (q * jnp.exp(b)).astype(BF16)
        to_mid = jnp.exp(jnp.minimum(b - mid, MAX_EXPONENT))
        qm_ref[rows, :] = (q * to_mid).astype(BF16)
        km_ref[rows, :] = (k / to_mid).astype(BF16)
        kd_ref[rows, :] = (k * jnp.exp(b_last - b)).astype(BF16)
        decay_ref[c] = jnp.broadcast_to(jnp.exp(b_last), decay_ref.shape[1:])
        return jnp.max(jnp.maximum(-mid, mid - b_last))

    def attend(c):
        rows = chunk_rows(c)
        decay = decay_ref[c][0:1, :]
        scores = [jnp.where(causal, _dot_nt(qm_ref[rows, ln], km_ref[rows, ln]), 0.0).astype(BF16)
                  for ln in head_lanes]
        v_t = [_dot_nt(eye, v_ref[rows, ln]).astype(BF16) for ln in head_lanes]
        o_state = [_dot_nt(qt_ref[rows, ln], st_ref[hh].astype(BF16)) for hh, ln in enumerate(head_lanes)]
        o_intra = [_dot(scores[hh], v_ref[rows, ln]) for hh, ln in enumerate(head_lanes)]
        incr = [_dot(v_t[hh], kd_ref[rows, ln]) for hh, ln in enumerate(head_lanes)]
        for hh, ln in enumerate(head_lanes):
            ostate_ref[:, ln] = o_state[hh]
            st_ref[hh] = st_ref[hh] * decay[:, ln] + incr[hh]
            opre_ref[c % 2, :, ln] = o_intra[hh] + o_state[hh]

    def attend_exact(c):
        rows = chunk_rows(c)
        k, b = gates(c)
        q = q_ref[rows, :]
        v = v_ref[rows, :].astype(F32)
        t_idx = lax.broadcasted_iota(jnp.int32, (CHUNK, 1), 0)

        def source_row(s_idx, acc):
            pick = lambda a: jnp.sum(jnp.where(t_idx == s_idx, a, 0.0), axis=0, keepdims=True)
            b_s, k_s, v_s = pick(b), pick(k), pick(v)
            w = jnp.where(t_idx >= s_idx, q * jnp.exp(jnp.minimum(b - b_s, 0.0)) * k_s, 0.0)
            return acc + jnp.concatenate(
                [jnp.sum(w[:, ln], axis=-1, keepdims=True) * v_s[:, ln] for ln in head_lanes], axis=1)

        opre_ref[c % 2] = lax.fori_loop(0, CHUNK, source_row, ostate_ref[...])

    def step(c, behind):
        largest_exponent = prepare(c)
        if behind:
            normalize_and_gate(c - 1)
        attend(c)

        @pl.when(largest_exponent > MAX_EXPONENT)
        def _():
            attend_exact(c)

    def body(c, carry):
        step(c, True)
        return carry

    step(0, False)
    lax.fori_loop(1, n_chunks, body, 0)
    normalize_and_gate(n_chunks - 1)

    o_ref[...] = _zero_pad_rows(x + _dot(y_ref[...], wout_ref[...]), l * tl, l_real)


def _odd_mixer(h, g, w_in, lb, gn_g, w_out, *, l_real):
    b, lp, d = h.shape
    heads = d // HGRN_HEAD_DIM
    tile = pl.BlockSpec((None, SEQ_TILE, d), lambda i, j: (i, j, 0))
    params = (g, w_in, lb, gn_g, w_out)
    big = lambda dt: pltpu.VMEM((SEQ_TILE, d), dt)
    return pl.pallas_call(
        functools.partial(_odd_kernel, l_real=l_real),
        grid=(b, lp // SEQ_TILE),
        in_specs=[tile] + [_param_spec(p) for p in params],
        out_specs=tile,
        out_shape=jax.ShapeDtypeStruct(h.shape, F32),
        scratch_shapes=[big(F32)] * 3 + [big(BF16)] * 7 + [
            pltpu.VMEM((heads, HGRN_HEAD_DIM, HGRN_HEAD_DIM), F32),
            pltpu.VMEM((SEQ_TILE // CHUNK, SUBLANES, d), F32),
            pltpu.VMEM((CHUNK, d), F32),
            pltpu.VMEM((2, CHUNK, d), F32),
            pltpu.VMEM((d // LANES, CHUNK // 2 + CHUNK, LANES), F32)],
        compiler_params=pltpu.CompilerParams(
            dimension_semantics=("arbitrary", "arbitrary"), vmem_limit_bytes=VMEM_LIMIT),
        name="odd_mixer",
    )(h, *map(_operand, params))


def kernel(x, meta_tokens, mix_norm_g, mlp_norm_g, final_norm_g, ev_w_in, ev_conv_w, ev_conv_b, ev_ln_g,
           ev_ln_b, ev_pool_w, ev_pool_b, ev_pool_scale, ev_w_out, od_w_in, od_gnorm_g, od_w_out, lb_param,
           mlp_w1, mlp_w2):
    bn, seq, d = x.shape
    depth = mix_norm_g.shape[0]
    l_real = N_META + seq
    l_pad = -(-l_real // SEQ_TILE) * SEQ_TILE

    lb_all = jnp.cumsum(jax.nn.softmax(lb_param.astype(F32), axis=0), axis=0)
    lb_all = lb_all - lb_all[0]

    row = lambda v: v.reshape(1, -1).astype(F32)
    ev_w_in, ev_pool_w, ev_w_out, od_w_in, od_w_out, mlp_w1, mlp_w2 = (
        w.astype(BF16) for w in (ev_w_in, ev_pool_w, ev_w_out, od_w_in, od_w_out, mlp_w1, mlp_w2))
    h = x.astype(F32)
    for layer in range(depth):
        j = layer // 2
        if layer % 2 == 0:
            h = _even_mixer(h, row(mix_norm_g[layer]), _layer_of(ev_w_in, j), _layer_of(ev_conv_w, j),
                            row(ev_conv_b[j]), row(ev_ln_g[j]), row(ev_ln_b[j]), _layer_of(ev_pool_w, j),
                            row(ev_pool_b[j]), row(ev_pool_scale[j]), _layer_of(ev_w_out, j),
                            l_real=l_real, l_pad=l_pad, meta=meta_tokens.astype(F32) if layer == 0 else None)
        else:
            h = _odd_mixer(h, row(mix_norm_g[layer]), _layer_of(od_w_in, j), row(lb_all[layer]),
                           row(od_gnorm_g[j]), _layer_of(od_w_out, j), l_real=l_real)
        w1, w2 = _layer_of(mlp_w1, layer), _layer_of(mlp_w2, layer)
        if layer == depth - 1:
            return _mlp_final(h, row(mlp_norm_g[layer]), row(final_norm_g), w1, w2, seq=seq)
        h = _mlp(h.reshape(bn * l_pad, d), row(mlp_norm_g[layer]), w1, w2).reshape(bn, l_pad, d)
```

```python
import functools

import jax
import jax.numpy as jnp
from jax import lax
from jax.experimental import pallas as pl
from jax.experimental.pallas import tpu as pltpu

F32 = jnp.float32
BF16 = jnp.bfloat16

N_META = 16
CONV_WIDTH = 31
POOL_WINDOWS = (2, 4, 8, 16)
HGRN_HEAD_DIM = 128
EPS = 1e-6
LANES = 128
SUBLANES = 8

CHUNK = 64
MAX_EXPONENT = 80.0
SEQ_TILE = 832
ROW_CHUNK = 64
CONV_HALO = 32
POOL_HALO = 16
FF_CHUNK = 1024
VMEM_LIMIT = 56 * 1024 * 1024

assert max(POOL_WINDOWS) - 1 <= POOL_HALO and CONV_WIDTH - 1 <= CONV_HALO


def _sigmoid(x):
    return 0.5 * jnp.tanh(0.5 * x) + 0.5


def _sigmoid_tail_exact(x):
    return 1.0 / (1.0 + jnp.exp(-x))


def _rmsnorm(x, g):
    return x * lax.rsqrt(jnp.mean(x * x, axis=-1, keepdims=True) + EPS) * g


def _dot(a, b):
    return jnp.dot(a, b, preferred_element_type=F32)


def _dot_nt(a, b):
    return lax.dot_general(a, b, (((1,), (1,)), ((), ())), preferred_element_type=F32)


def _zero_pad_rows(y, first_row, l_real):
    row = first_row + lax.broadcasted_iota(jnp.int32, (y.shape[0], 1), 0)
    return jnp.where(row < l_real, y, 0.0)


def _resident(shape):
    nd = len(shape)
    return pl.BlockSpec(shape, lambda *_: (0,) * nd, pipeline_mode=pl.Buffered(1))


def _layer_of(stacked, layer):
    return stacked, layer


def _operand(p):
    return p[0] if isinstance(p, tuple) else p


def _param_spec(p):
    if not isinstance(p, tuple):
        return _resident(p.shape)
    stacked, layer = p
    nd = stacked.ndim
    return pl.BlockSpec((None,) + stacked.shape[1:], lambda *_: (layer,) + (0,) * (nd - 1),
                        pipeline_mode=pl.Buffered(1))


def _param_shape(p):
    return p[0].shape[1:] if isinstance(p, tuple) else p.shape


def _mlp_rows(x, g_ref, w1_ref, w2_ref):
    n = _rmsnorm(x, g_ref[...]).astype(BF16)
    acc = x
    for c in range(w1_ref.shape[1] // FF_CHUNK):
        cols = slice(c * FF_CHUNK, (c + 1) * FF_CHUNK)
        hid = _dot(n, w1_ref[:, cols])
        hid = jnp.square(jnp.maximum(hid, 0.0)).astype(BF16)
        acc = acc + _dot(hid, w2_ref[cols, :])
    return acc


def _mlp_kernel(h_ref, g_ref, w1_ref, w2_ref, o_ref):
    half = h_ref.shape[0] // 2
    for rows in (slice(0, half), slice(half, 2 * half)):
        o_ref[rows, :] = _mlp_rows(h_ref[rows, :], g_ref, w1_ref, w2_ref)


def _mlp(h2d, g, w1, w2):
    rows, d = h2d.shape
    tile = pl.BlockSpec((SEQ_TILE, d), lambda i: (i, 0))
    return pl.pallas_call(
        _mlp_kernel,
        grid=(rows // SEQ_TILE,),
        in_specs=[tile, _resident(g.shape), _param_spec(w1), _param_spec(w2)],
        out_specs=tile,
        out_shape=jax.ShapeDtypeStruct(h2d.shape, F32),
        compiler_params=pltpu.CompilerParams(
            dimension_semantics=("arbitrary",), vmem_limit_bytes=VMEM_LIMIT),
        name="mlp",
    )(h2d, g, _operand(w1), _operand(w2))


def _mlp_final_kernel(h_ref, hnext_ref, g_ref, fg_ref, w1_ref, w2_ref, o_ref):
    half = o_ref.shape[0] // 2
    x_lo = h_ref[N_META:N_META + half, :]
    x_hi = jnp.concatenate([h_ref[N_META + half:, :], hnext_ref[0:N_META, :]], axis=0)
    o_ref[0:half, :] = _rmsnorm(_mlp_rows(x_lo, g_ref, w1_ref, w2_ref), fg_ref[...])
    o_ref[half:, :] = _rmsnorm(_mlp_rows(x_hi, g_ref, w1_ref, w2_ref), fg_ref[...])


def _mlp_final(h, g, fg, w1, w2, *, seq):
    b, lp, d = h.shape
    last = lp // SEQ_TILE - 1
    return pl.pallas_call(
        _mlp_final_kernel,
        grid=(b, pl.cdiv(seq, SEQ_TILE)),
        in_specs=[pl.BlockSpec((None, SEQ_TILE, d), lambda i, j: (i, j, 0)),
                  pl.BlockSpec((None, SEQ_TILE, d), lambda i, j: (i, jnp.minimum(j + 1, last), 0)),
                  _resident(g.shape), _resident(fg.shape), _param_spec(w1), _param_spec(w2)],
        out_specs=pl.BlockSpec((None, SEQ_TILE, d), lambda i, j: (i, j, 0)),
        out_shape=jax.ShapeDtypeStruct((b, seq, d), F32),
        compiler_params=pltpu.CompilerParams(
            dimension_semantics=("arbitrary", "arbitrary"), vmem_limit_bytes=VMEM_LIMIT),
        name="mlp_final",
    )(h, h, g, fg, _operand(w1), _operand(w2))


def _even_kernel(*refs, l_real, from_inputs):
    if from_inputs:
        meta_ref, xprev_ref, xcur_ref = refs[:3]
        refs = refs[2:]
    (h_ref, g_ref, win_ref, cw_ref, cb_ref, lng_ref, lnb_ref, pw_ref, pb_ref, ps_ref, wout_ref, o_ref,
     a_ext, p_ext, conv_ref, d_ref, y_ref) = refs
    tl = o_ref.shape[0]
    dc = cw_ref.shape[1]
    gd = pw_ref.shape[1]
    l = pl.program_id(1)

    if from_inputs:
        head = jnp.where(l == 0, meta_ref[...], xprev_ref[tl - N_META:, :])
        x = _zero_pad_rows(jnp.concatenate([head, xcur_ref[0:tl - N_META, :]], axis=0), l * tl, l_real)
    else:
        x = h_ref[...]
    n = _rmsnorm(x, g_ref[...]).astype(BF16)
    u = _dot(n, win_ref[...])

    @pl.when(l == 0)
    def _():
        a_ext[:, 0:CONV_HALO, :] = jnp.zeros((a_ext.shape[0], CONV_HALO, LANES), F32)
        p_ext[:, 0:POOL_HALO, :] = jnp.zeros((p_ext.shape[0], POOL_HALO, LANES), F32)

    @pl.when(l > 0)
    def _():
        a_ext[:, 0:CONV_HALO, :] = a_ext[:, tl:tl + CONV_HALO, :]
        p_ext[:, 0:POOL_HALO, :] = p_ext[:, tl:tl + POOL_HALO, :]

    a = u[:, 0:dc] * _sigmoid(u[:, dc:2 * dc])
    for lt in range(dc // LANES):
        a_ext[lt, CONV_HALO:CONV_HALO + tl, :] = a[:, lt * LANES:(lt + 1) * LANES]
    for gi in range(len(POOL_WINDOWS)):
        p_ext[gi, POOL_HALO:POOL_HALO + tl, :] = u[:, 2 * dc + gi * gd:2 * dc + (gi + 1) * gd]

    def chunk_rows(r):
        return pl.multiple_of(r * ROW_CHUNK, ROW_CHUNK)

    def accumulate(r):
        r0 = chunk_rows(r)
        for lt in range(dc // LANES):
            lanes = slice(lt * LANES, (lt + 1) * LANES)
            part = jnp.broadcast_to(cb_ref[:, lanes], (ROW_CHUNK, LANES))
            for j in range(CONV_WIDTH):
                off = CONV_HALO - (CONV_WIDTH - 1) + j
                part = part + cw_ref[j:j + 1, lanes] * a_ext[lt, pl.ds(r0 + off, ROW_CHUNK), :]
            conv_ref[pl.ds(r0, ROW_CHUNK), lanes] = part

        pos = l * tl + r0 + lax.broadcasted_iota(jnp.int32, (ROW_CHUNK, 1), 0)
        for gi, w in enumerate(POOL_WINDOWS):
            cur = p_ext[gi, pl.ds(r0 + POOL_HALO, ROW_CHUNK), :]
            s = cur
            for j in range(1, w):
                s = s + p_ext[gi, pl.ds(r0 + POOL_HALO - j, ROW_CHUNK), :]
            cnt = jnp.minimum(pos + 1, w).astype(F32)
            d_ref[pl.ds(r0, ROW_CHUNK), gi * gd:(gi + 1) * gd] = (s / cnt - cur).astype(BF16)

    def normalize(r):
        rows = pl.ds(chunk_rows(r), ROW_CHUNK)
        acc = conv_ref[rows, :]
        mu = jnp.mean(acc, axis=-1, keepdims=True)
        xc = acc - mu
        yn = xc * lax.rsqrt(jnp.mean(xc * xc, axis=-1, keepdims=True) + EPS) * lng_ref[...] + lnb_ref[...]
        y_ref[rows, 0:dc] = (yn * _sigmoid(yn)).astype(BF16)

    def chunk(r, carry):
        accumulate(r)
        normalize(r - 1)
        return carry

    n_row_chunks = tl // ROW_CHUNK
    accumulate(0)
    lax.fori_loop(1, n_row_chunks, chunk, 0)
    normalize(n_row_chunks - 1)

    for gi in range(len(POOL_WINDOWS)):
        lanes = slice(gi * gd, (gi + 1) * gd)
        yb = (_dot(d_ref[:, lanes], pw_ref[gi]) + pb_ref[:, lanes]) * ps_ref[:, lanes]
        y_ref[:, dc + gi * gd:dc + (gi + 1) * gd] = yb.astype(BF16)

    o_ref[...] = _zero_pad_rows(x + _dot(y_ref[...], wout_ref[...]), l * tl, l_real)


def _even_mixer(h, g, w_in, conv_w, conv_b, ln_g, ln_b, pool_w, pool_b, pool_scale, w_out, *, l_real, l_pad,
                meta=None):
    b, _, d = h.shape
    dc = _param_shape(conv_w)[1]
    dp = pool_scale.shape[1]
    assert dc % LANES == 0 and _param_shape(pool_w)[1:] == (LANES, LANES)
    tile = pl.BlockSpec((None, SEQ_TILE, d), lambda i, j: (i, j, 0))
    params = (g, w_in, conv_w, conv_b, ln_g, ln_b, pool_w, pool_b, pool_scale, w_out)
    if meta is None:
        stream_in, stream_specs = (h,), [tile]
    else:
        prev = pl.BlockSpec((None, SEQ_TILE, d), lambda i, j: (i, jnp.maximum(j - 1, 0), 0))
        stream_in, stream_specs = (meta, h, h), [_resident(meta.shape), prev, tile]
    return pl.pallas_call(
        functools.partial(_even_kernel, l_real=l_real, from_inputs=meta is not None),
        grid=(b, l_pad // SEQ_TILE),
        in_specs=stream_specs + [_param_spec(p) for p in params],
        out_specs=tile,
        out_shape=jax.ShapeDtypeStruct((b, l_pad, d), F32),
        scratch_shapes=[
            pltpu.VMEM((dc // LANES, CONV_HALO + SEQ_TILE, LANES), F32),
            pltpu.VMEM((len(POOL_WINDOWS), POOL_HALO + SEQ_TILE, LANES), F32),
            pltpu.VMEM((SEQ_TILE, dc), F32),
            pltpu.VMEM((SEQ_TILE, dp), BF16),
            pltpu.VMEM((SEQ_TILE, dc + dp), BF16),
        ],
        compiler_params=pltpu.CompilerParams(
            dimension_semantics=("arbitrary", "arbitrary"), vmem_limit_bytes=VMEM_LIMIT),
        name="even_mixer",
    )(*stream_in, *map(_operand, params))


def _chunk_cumsum(x, scan_ref):
    rows, n = x.shape
    halo = rows // 2
    parts = [x[:, lt * LANES:(lt + 1) * LANES] for lt in range(n // LANES)]
    shift = 1
    while shift < rows:
        if shift % SUBLANES:
            for lt, part in enumerate(parts):
                scan_ref[lt, halo:halo + rows, :] = part
            parts = [part + scan_ref[lt, halo - shift:halo - shift + rows, :] for lt, part in enumerate(parts)]
        else:
            pad = jnp.zeros((shift, LANES), F32)
            parts = [part + jnp.concatenate([pad, part[:rows - shift, :]], axis=0) for part in parts]
        shift *= 2
    return jnp.concatenate(parts, axis=1)


def _odd_kernel(h_ref, g_ref, win_ref, lb_ref, gng_ref, wout_ref, o_ref,
                q_ref, logf_ref, k_ref, v_ref, gate_ref, qt_ref, qm_ref, km_ref, kd_ref, y_ref, st_ref, decay_ref,
                ostate_ref, opre_ref, scan_ref, *, l_real):
    tl, d = h_ref.shape
    hd = HGRN_HEAD_DIM
    heads = d // hd
    l = pl.program_id(1)

    @pl.when(l == 0)
    def _():
        st_ref[...] = jnp.zeros(st_ref.shape, F32)
        scan_ref[:, 0:CHUNK // 2, :] = jnp.zeros((scan_ref.shape[0], CHUNK // 2, LANES), F32)

    x = h_ref[...]
    n = _rmsnorm(x, g_ref[...]).astype(BF16)
    q = _dot(n, win_ref[:, 0:d])
    q_ref[...] = q * _sigmoid(q)
    lb = lb_ref[...]
    forget = lb + (1.0 - lb) * _sigmoid_tail_exact(_dot(n, win_ref[:, d:2 * d]))
    logf_ref[...] = jnp.log(forget)
    k_ref[...] = 1.0 - forget
    v_ref[...] = _dot(n, win_ref[:, 2 * d:3 * d]).astype(BF16)
    gt = _dot(n, win_ref[:, 3 * d:4 * d])
    gate_ref[...] = (gt * _sigmoid(gt)).astype(BF16)

    causal = (lax.broadcasted_iota(jnp.int32, (CHUNK, CHUNK), 0)
              >= lax.broadcasted_iota(jnp.int32, (CHUNK, CHUNK), 1))
    eye = (lax.broadcasted_iota(jnp.int32, (hd, hd), 0)
           == lax.broadcasted_iota(jnp.int32, (hd, hd), 1)).astype(BF16)
    head_lanes = [slice(hh * hd, (hh + 1) * hd) for hh in range(heads)]
    n_chunks = tl // CHUNK

    def chunk_rows(c):
        return pl.ds(pl.multiple_of(c * CHUNK, CHUNK), CHUNK)

    def gates(c):
        rows = chunk_rows(c)
        return k_ref[rows, :], _chunk_cumsum(logf_ref[rows, :], scan_ref)

    def normalize_and_gate(c):
        rows = chunk_rows(c)
        for ln in head_lanes:
            o = opre_ref[c % 2, :, ln]
            on = o * lax.rsqrt(jnp.mean(o * o, axis=-1, keepdims=True) + EPS) * gng_ref[...]
            y_ref[rows, ln] = (on * gate_ref[rows, ln].astype(F32)).astype(BF16)

    def prepare(c):
        rows = chunk_rows(c)
        k, b = gates(c)
        q = q_ref[rows, :]
        mid = b[CHUNK // 2 - 1:CHUNK // 2, :]
        b_last = b[CHUNK - 1:CHUNK, :]
        qt_ref[rows, :] = (q * jnp.exp(b)).astype(BF16)
        to_mid = jnp.exp(jnp.minimum(b - mid, MAX_EXPONENT))
        qm_ref[rows, :] = (q * to_mid).astype(BF16)
        km_ref[rows, :] = (k / to_mid).astype(BF16)
        kd_ref[rows, :] = (k * jnp.exp(b_last - b)).astype(BF16)
        decay_ref[c] = jnp.broadcast_to(jnp.exp(b_last), decay_ref.shape[1:])
        return jnp.max(jnp.maximum(-mid, mid - b_last))

    def attend(c):
        rows = chunk_rows(c)
        decay = decay_ref[c][0:1, :]
        scores = [jnp.where(causal, _dot_nt(qm_ref[rows, ln], km_ref[rows, ln]), 0.0).astype(BF16)
                  for ln in head_lanes]
        v_t = [_dot_nt(eye, v_ref[rows, ln]).astype(BF16) for ln in head_lanes]
        o_state = [_dot_nt(qt_ref[rows, ln], st_ref[hh].astype(BF16)) for hh, ln in enumerate(head_lanes)]
        o_intra = [_dot(scores[hh], v_ref[rows, ln]) for hh, ln in enumerate(head_lanes)]
        incr = [_dot(v_t[hh], kd_ref[rows, ln]) for hh, ln in enumerate(head_lanes)]
        for hh, ln in enumerate(head_lanes):
            ostate_ref[:, ln] = o_state[hh]
            st_ref[hh] = st_ref[hh] * decay[:, ln] + incr[hh]
            opre_ref[c % 2, :, ln] = o_intra[hh] + o_state[hh]

    def attend_exact(c):
        rows = chunk_rows(c)
        k, b = gates(c)
        q = q_ref[rows, :]
        v = v_ref[rows, :].astype(F32)
        t_idx = lax.broadcasted_iota(jnp.int32, (CHUNK, 1), 0)

        def source_row(s_idx, acc):
            pick = lambda a: jnp.sum(jnp.where(t_idx == s_idx, a, 0.0), axis=0, keepdims=True)
            b_s, k_s, v_s = pick(b), pick(k), pick(v)
            w = jnp.where(t_idx >= s_idx, q * jnp.exp(jnp.minimum(b - b_s, 0.0)) * k_s, 0.0)
            return acc + jnp.concatenate(
                [jnp.sum(w[:, ln], axis=-1, keepdims=True) * v_s[:, ln] for ln in head_lanes], axis=1)

        opre_ref[c % 2] = lax.fori_loop(0, CHUNK, source_row, ostate_ref[...])

    def step(c, behind):
        largest_exponent = prepare(c)
        if behind:
            normalize_and_gate(c - 1)
        attend(c)

        @pl.when(largest_exponent > MAX_EXPONENT)
        def _():
            attend_exact(c)

    def body(c, carry):
        step(c, True)
        return carry

    step(0, False)
    lax.fori_loop(1, n_chunks, body, 0)
    normalize_and_gate(n_chunks - 1)

    o_ref[...] = _zero_pad_rows(x + _dot(y_ref[...], wout_ref[...]), l * tl, l_real)


def _odd_mixer(h, g, w_in, lb, gn_g, w_out, *, l_real):
    b, lp, d = h.shape
    heads = d // HGRN_HEAD_DIM
    tile = pl.BlockSpec((None, SEQ_TILE, d), lambda i, j: (i, j, 0))
    params = (g, w_in, lb, gn_g, w_out)
    big = lambda dt: pltpu.VMEM((SEQ_TILE, d), dt)
    return pl.pallas_call(
        functools.partial(_odd_kernel, l_real=l_real),
        grid=(b, lp // SEQ_TILE),
        in_specs=[tile] + [_param_spec(p) for p in params],
        out_specs=tile,
        out_shape=jax.ShapeDtypeStruct(h.shape, F32),
        scratch_shapes=[big(F32)] * 3 + [big(BF16)] * 7 + [
            pltpu.VMEM((heads, HGRN_HEAD_DIM, HGRN_HEAD_DIM), F32),
            pltpu.VMEM((SEQ_TILE // CHUNK, SUBLANES, d), F32),
            pltpu.VMEM((CHUNK, d), F32),
            pltpu.VMEM((2, CHUNK, d), F32),
            pltpu.VMEM((d // LANES, CHUNK // 2 + CHUNK, LANES), F32)],
        compiler_params=pltpu.CompilerParams(
            dimension_semantics=("arbitrary", "arbitrary"), vmem_limit_bytes=VMEM_LIMIT),
        name="odd_mixer",
    )(h, *map(_operand, params))


def kernel(x, meta_tokens, mix_norm_g, mlp_norm_g, final_norm_g, ev_w_in, ev_conv_w, ev_conv_b, ev_ln_g,
           ev_ln_b, ev_pool_w, ev_pool_b, ev_pool_scale, ev_w_out, od_w_in, od_gnorm_g, od_w_out, lb_param,
           mlp_w1, mlp_w2):
    bn, seq, d = x.shape
    depth = mix_norm_g.shape[0]
    l_real = N_META + seq
    l_pad = -(-l_real // SEQ_TILE) * SEQ_TILE

    lb_all = jnp.cumsum(jax.nn.softmax(lb_param.astype(F32), axis=0), axis=0)
    lb_all = lb_all - lb_all[0]

    row = lambda v: v.reshape(1, -1).astype(F32)
    ev_w_in, ev_pool_w, ev_w_out, od_w_in, od_w_out, mlp_w1, mlp_w2 = (
        w.astype(BF16) for w in (ev_w_in, ev_pool_w, ev_w_out, od_w_in, od_w_out, mlp_w1, mlp_w2))
    h = x.astype(F32)
    for layer in range(depth):
        j = layer // 2
        if layer % 2 == 0:
            h = _even_mixer(h, row(mix_norm_g[layer]), _layer_of(ev_w_in, j), _layer_of(ev_conv_w, j),
                            row(ev_conv_b[j]), row(ev_ln_g[j]), row(ev_ln_b[j]), _layer_of(ev_pool_w, j),
                            row(ev_pool_b[j]), row(ev_pool_scale[j]), _layer_of(ev_w_out, j),
                            l_real=l_real, l_pad=l_pad, meta=meta_tokens.astype(F32) if layer == 0 else None)
        else:
            h = _odd_mixer(h, row(mix_norm_g[layer]), _layer_of(od_w_in, j), row(lb_all[layer]),
                           row(od_gnorm_g[j]), _layer_of(od_w_out, j), l_real=l_real)
        w1, w2 = _layer_of(mlp_w1, layer), _layer_of(mlp_w2, layer)
        if layer == depth - 1:
            return _mlp_final(h, row(mlp_norm_g[layer]), row(final_norm_g), w1, w2, seq=seq)
        h = _mlp(h.reshape(bn * l_pad, d), row(mlp_norm_g[layer]), w1, w2).reshape(bn, l_pad, d)
```

```python
import functools

import jax
import jax.numpy as jnp
from jax import lax
from jax.experimental import pallas as pl
from jax.experimental.pallas import tpu as pltpu

F32 = jnp.float32
BF16 = jnp.bfloat16

N_META = 16
CONV_WIDTH = 31
POOL_WINDOWS = (2, 4, 8, 16)
HGRN_HEAD_DIM = 128
EPS = 1e-6
LANES = 128
SUBLANES = 8

CHUNK = 64
MAX_EXPONENT = 80.0
SEQ_TILE = 832
ROW_CHUNK = 64
CONV_HALO = 32
POOL_HALO = 16
FF_CHUNK = 1024
CAST_STEPS = 16
VMEM_LIMIT = 56 * 1024 * 1024

assert max(POOL_WINDOWS) - 1 <= POOL_HALO and CONV_WIDTH - 1 <= CONV_HALO


def _sigmoid(x):
    return 0.5 * jnp.tanh(0.5 * x) + 0.5


def _sigmoid_tail_exact(x):
    return 1.0 / (1.0 + jnp.exp(-x))


def _rmsnorm(x, g):
    return x * lax.rsqrt(jnp.mean(x * x, axis=-1, keepdims=True) + EPS) * g


def _dot(a, b):
    return jnp.dot(a, b, preferred_element_type=F32)


def _dot_nt(a, b):
    return lax.dot_general(a, b, (((1,), (1,)), ((), ())), preferred_element_type=F32)


def _zero_pad_rows(y, first_row, l_real):
    row = first_row + lax.broadcasted_iota(jnp.int32, (y.shape[0], 1), 0)
    return jnp.where(row < l_real, y, 0.0)


def _resident(shape):
    nd = len(shape)
    return pl.BlockSpec(shape, lambda *_: (0,) * nd, pipeline_mode=pl.Buffered(1))


def _layer_of(stacked, layer):
    return stacked, layer


def _operand(p):
    return p[0] if isinstance(p, tuple) else p


def _param_spec(p):
    if not isinstance(p, tuple):
        return _resident(p.shape)
    stacked, layer = p
    nd = stacked.ndim
    return pl.BlockSpec((None,) + stacked.shape[1:], lambda *_: (layer,) + (0,) * (nd - 1),
                        pipeline_mode=pl.Buffered(1))


def _param_shape(p):
    return p[0].shape[1:] if isinstance(p, tuple) else p.shape


def _mlp_rows(x, g_ref, w1_ref, w2_ref):
    n = _rmsnorm(x, g_ref[...]).astype(BF16)
    acc = x
    for c in range(w1_ref.shape[1] // FF_CHUNK):
        cols = slice(c * FF_CHUNK, (c + 1) * FF_CHUNK)
        hid = _dot(n, w1_ref[:, cols])
        hid = jnp.square(jnp.maximum(hid, 0.0)).astype(BF16)
        acc = acc + _dot(hid, w2_ref[cols, :])
    return acc


def _mlp_kernel(h_ref, g_ref, w1_ref, w2_ref, o_ref):
    o_ref[...] = _mlp_rows(h_ref[...], g_ref, w1_ref, w2_ref)


def _mlp(h2d, g, w1, w2):
    rows, d = h2d.shape
    tile = pl.BlockSpec((SEQ_TILE, d), lambda i: (i, 0))
    return pl.pallas_call(
        _mlp_kernel,
        grid=(rows // SEQ_TILE,),
        in_specs=[tile, _resident(g.shape), _param_spec(w1), _param_spec(w2)],
        out_specs=tile,
        out_shape=jax.ShapeDtypeStruct(h2d.shape, F32),
        compiler_params=pltpu.CompilerParams(
            dimension_semantics=("arbitrary",), vmem_limit_bytes=VMEM_LIMIT),
        name="mlp",
    )(h2d, g, _operand(w1), _operand(w2))


def _mlp_final_kernel(h_ref, hnext_ref, g_ref, fg_ref, w1_ref, w2_ref, o_ref):
    x = jnp.concatenate([h_ref[N_META:, :], hnext_ref[0:N_META, :]], axis=0)
    o_ref[...] = _rmsnorm(_mlp_rows(x, g_ref, w1_ref, w2_ref), fg_ref[...])


def _mlp_final(h, g, fg, w1, w2, *, seq):
    b, lp, d = h.shape
    last = lp // SEQ_TILE - 1
    return pl.pallas_call(
        _mlp_final_kernel,
        grid=(b, pl.cdiv(seq, SEQ_TILE)),
        in_specs=[pl.BlockSpec((None, SEQ_TILE, d), lambda i, j: (i, j, 0)),
                  pl.BlockSpec((None, SEQ_TILE, d), lambda i, j: (i, jnp.minimum(j + 1, last), 0)),
                  _resident(g.shape), _resident(fg.shape), _param_spec(w1), _param_spec(w2)],
        out_specs=pl.BlockSpec((None, SEQ_TILE, d), lambda i, j: (i, j, 0)),
        out_shape=jax.ShapeDtypeStruct((b, seq, d), F32),
        compiler_params=pltpu.CompilerParams(
            dimension_semantics=("arbitrary", "arbitrary"), vmem_limit_bytes=VMEM_LIMIT),
        name="mlp_final",
    )(h, h, g, fg, _operand(w1), _operand(w2))


def _even_kernel(*refs, l_real, from_inputs, n_cast):
    if from_inputs:
        meta_ref, xprev_ref, xcur_ref = refs[:3]
        refs = refs[2:]
    (h_ref, g_ref, win_ref, cw_ref, cb_ref, lng_ref, lnb_ref, pw_ref, pb_ref, ps_ref, wout_ref) = refs[:11]
    cast_in = refs[11:11 + n_cast]
    o_ref = refs[11 + n_cast]
    cast_out = refs[12 + n_cast:12 + 2 * n_cast]
    a_ext, p_ext, conv_ref, d_ref, y_ref = refs[12 + 2 * n_cast:]

    for w_in, w_out in zip(cast_in, cast_out):
        w_out[...] = w_in[...].astype(BF16)

    tl = o_ref.shape[0]
    dc = cw_ref.shape[1]
    gd = pw_ref.shape[1]
    l = pl.program_id(1)

    if from_inputs:
        head = jnp.where(l == 0, meta_ref[...], xprev_ref[tl - N_META:, :])
        x = _zero_pad_rows(jnp.concatenate([head, xcur_ref[0:tl - N_META, :]], axis=0), l * tl, l_real)
    else:
        x = h_ref[...]
    n = _rmsnorm(x, g_ref[...]).astype(BF16)
    u = _dot(n, win_ref[...])

    @pl.when(l == 0)
    def _():
        a_ext[:, 0:CONV_HALO, :] = jnp.zeros((a_ext.shape[0], CONV_HALO, LANES), F32)
        p_ext[:, 0:POOL_HALO, :] = jnp.zeros((p_ext.shape[0], POOL_HALO, LANES), F32)

    @pl.when(l > 0)
    def _():
        a_ext[:, 0:CONV_HALO, :] = a_ext[:, tl:tl + CONV_HALO, :]
        p_ext[:, 0:POOL_HALO, :] = p_ext[:, tl:tl + POOL_HALO, :]

    a = u[:, 0:dc] * _sigmoid(u[:, dc:2 * dc])
    for lt in range(dc // LANES):
        a_ext[lt, CONV_HALO:CONV_HALO + tl, :] = a[:, lt * LANES:(lt + 1) * LANES]
    for gi in range(len(POOL_WINDOWS)):
        p_ext[gi, POOL_HALO:POOL_HALO + tl, :] = u[:, 2 * dc + gi * gd:2 * dc + (gi + 1) * gd]

    def chunk_rows(r):
        return pl.multiple_of(r * ROW_CHUNK, ROW_CHUNK)

    def accumulate(r):
        r0 = chunk_rows(r)
        for lt in range(dc // LANES):
            lanes = slice(lt * LANES, (lt + 1) * LANES)
            part = jnp.broadcast_to(cb_ref[:, lanes], (ROW_CHUNK, LANES))
            for j in range(CONV_WIDTH):
                off = CONV_HALO - (CONV_WIDTH - 1) + j
                part = part + cw_ref[j:j + 1, lanes] * a_ext[lt, pl.ds(r0 + off, ROW_CHUNK), :]
            conv_ref[pl.ds(r0, ROW_CHUNK), lanes] = part

        pos = l * tl + r0 + lax.broadcasted_iota(jnp.int32, (ROW_CHUNK, 1), 0)
        for gi, w in enumerate(POOL_WINDOWS):
            cur = p_ext[gi, pl.ds(r0 + POOL_HALO, ROW_CHUNK), :]
            s = cur
            for j in range(1, w):
                s = s + p_ext[gi, pl.ds(r0 + POOL_HALO - j, ROW_CHUNK), :]
            cnt = jnp.minimum(pos + 1, w).astype(F32)
            d_ref[pl.ds(r0, ROW_CHUNK), gi * gd:(gi + 1) * gd] = (s / cnt - cur).astype(BF16)

    def normalize(r):
        rows = pl.ds(chunk_rows(r), ROW_CHUNK)
        acc = conv_ref[rows, :]
        mu = jnp.mean(acc, axis=-1, keepdims=True)
        xc = acc - mu
        yn = xc * lax.rsqrt(jnp.mean(xc * xc, axis=-1, keepdims=True) + EPS) * lng_ref[...] + lnb_ref[...]
        y_ref[rows, 0:dc] = (yn * _sigmoid(yn)).astype(BF16)

    def chunk(r, carry):
        accumulate(r)
        normalize(r - 1)
        return carry

    n_row_chunks = tl // ROW_CHUNK
    accumulate(0)
    lax.fori_loop(1, n_row_chunks, chunk, 0)
    normalize(n_row_chunks - 1)

    for gi in range(len(POOL_WINDOWS)):
        lanes = slice(gi * gd, (gi + 1) * gd)
        yb = (_dot(d_ref[:, lanes], pw_ref[gi]) + pb_ref[:, lanes]) * ps_ref[:, lanes]
        y_ref[:, dc + gi * gd:dc + (gi + 1) * gd] = yb.astype(BF16)

    o_ref[...] = _zero_pad_rows(x + _dot(y_ref[...], wout_ref[...]), l * tl, l_real)


def _even_mixer(h, g, w_in, conv_w, conv_b, ln_g, ln_b, pool_w, pool_b, pool_scale, w_out, *, l_real, l_pad,
                meta=None, to_bf16=()):
    b, _, d = h.shape
    n_l = l_pad // SEQ_TILE
    assert b * n_l >= CAST_STEPS
    block_of_step = lambda i, j: jnp.minimum(i * n_l + j, CAST_STEPS - 1)
    cast_in, cast_in_specs, cast_out_specs, cast_out_shapes = [], [], [], []
    for stacked, layer in to_bf16:
        rows, cols = stacked.shape[1] // CAST_STEPS, stacked.shape[2]
        cast_in.append(stacked)
        cast_in_specs.append(
            pl.BlockSpec((None, rows, cols), lambda i, j, layer=layer: (layer, block_of_step(i, j), 0)))
        cast_out_specs.append(pl.BlockSpec((rows, cols), lambda i, j: (block_of_step(i, j), 0)))
        cast_out_shapes.append(jax.ShapeDtypeStruct(stacked.shape[1:], BF16))
    dc = _param_shape(conv_w)[1]
    dp = pool_scale.shape[1]
    assert dc % LANES == 0 and _param_shape(pool_w)[1:] == (LANES, LANES)
    tile = pl.BlockSpec((None, SEQ_TILE, d), lambda i, j: (i, j, 0))
    params = (g, w_in, conv_w, conv_b, ln_g, ln_b, pool_w, pool_b, pool_scale, w_out)
    if meta is None:
        stream_in, stream_specs = (h,), [tile]
    else:
        prev = pl.BlockSpec((None, SEQ_TILE, d), lambda i, j: (i, jnp.maximum(j - 1, 0), 0))
        stream_in, stream_specs = (meta, h, h), [_resident(meta.shape), prev, tile]
    out = pl.pallas_call(
        functools.partial(_even_kernel, l_real=l_real, from_inputs=meta is not None, n_cast=len(cast_in)),
        grid=(b, n_l),
        in_specs=stream_specs + [_param_spec(p) for p in params] + cast_in_specs,
        out_specs=[tile] + cast_out_specs,
        out_shape=[jax.ShapeDtypeStruct((b, l_pad, d), F32)] + cast_out_shapes,
        scratch_shapes=[
            pltpu.VMEM((dc // LANES, CONV_HALO + SEQ_TILE, LANES), F32),
            pltpu.VMEM((len(POOL_WINDOWS), POOL_HALO + SEQ_TILE, LANES), F32),
            pltpu.VMEM((SEQ_TILE, dc), F32),
            pltpu.VMEM((SEQ_TILE, dp), BF16),
            pltpu.VMEM((SEQ_TILE, dc + dp), BF16),
        ],
        compiler_params=pltpu.CompilerParams(
            dimension_semantics=("arbitrary", "arbitrary"), vmem_limit_bytes=VMEM_LIMIT),
        name="even_mixer",
    )(*stream_in, *map(_operand, params), *cast_in)
    return out[0], out[1:]


def _chunk_cumsum(x):
    rows, n = x.shape
    row = lax.broadcasted_iota(jnp.int32, (rows, 1), 0)
    shift = 1
    while shift < rows:
        if shift % SUBLANES:
            x = x + jnp.where(row >= shift, pltpu.roll(x, shift, axis=0), 0.0)
        else:
            x = x + jnp.concatenate([jnp.zeros((shift, n), F32), x[:rows - shift, :]], axis=0)
        shift *= 2
    return x


def _odd_kernel(h_ref, g_ref, win_ref, lb_ref, gng_ref, wout_ref, o_ref,
                q_ref, logf_ref, k_ref, v_ref, gate_ref, y_ref, st_ref, ostate_ref, opre_ref, *, l_real):
    tl, d = h_ref.shape
    hd = HGRN_HEAD_DIM
    heads = d // hd
    l = pl.program_id(1)

    @pl.when(l == 0)
    def _():
        st_ref[...] = jnp.zeros(st_ref.shape, F32)

    x = h_ref[...]
    n = _rmsnorm(x, g_ref[...]).astype(BF16)
    q = _dot(n, win_ref[:, 0:d])
    q_ref[...] = q * _sigmoid(q)
    lb = lb_ref[...]
    forget = lb + (1.0 - lb) * _sigmoid_tail_exact(_dot(n, win_ref[:, d:2 * d]))
    logf_ref[...] = jnp.log(forget)
    k_ref[...] = 1.0 - forget
    v_ref[...] = _dot(n, win_ref[:, 2 * d:3 * d]).astype(BF16)
    gt = _dot(n, win_ref[:, 3 * d:4 * d])
    gate_ref[...] = (gt * _sigmoid(gt)).astype(BF16)

    causal = (lax.broadcasted_iota(jnp.int32, (CHUNK, CHUNK), 0)
              >= lax.broadcasted_iota(jnp.int32, (CHUNK, CHUNK), 1))
    eye = (lax.broadcasted_iota(jnp.int32, (hd, hd), 0)
           == lax.broadcasted_iota(jnp.int32, (hd, hd), 1)).astype(BF16)
    head_lanes = [slice(hh * hd, (hh + 1) * hd) for hh in range(heads)]
    n_chunks = tl // CHUNK

    def chunk_rows(c):
        return pl.ds(pl.multiple_of(c * CHUNK, CHUNK), CHUNK)

    def gates(c):
        rows = chunk_rows(c)
        return k_ref[rows, :], _chunk_cumsum(logf_ref[rows, :])

    def normalize_and_gate(c):
        rows = chunk_rows(c)
        for ln in head_lanes:
            o = opre_ref[c % 2, :, ln]
            on = o * lax.rsqrt(jnp.mean(o * o, axis=-1, keepdims=True) + EPS) * gng_ref[...]
            y_ref[rows, ln] = (on * gate_ref[rows, ln].astype(F32)).astype(BF16)

    def prepare(c):
        rows = chunk_rows(c)
        k, b = gates(c)
        q = q_ref[rows, :]
        mid = b[CHUNK // 2 - 1:CHUNK // 2, :]
        b_last = b[CHUNK - 1:CHUNK, :]
        qt = (q * jnp.exp(b)).astype(BF16)
        to_mid = jnp.exp(jnp.minimum(b - mid, MAX_EXPONENT))
        qm = (q * to_mid).astype(BF16)
        km = (k / to_mid).astype(BF16)
        kd = (k * jnp.exp(b_last - b)).astype(BF16)
        return (qt, qm, km, kd, jnp.exp(b_last)), jnp.max(jnp.maximum(-mid, mid - b_last))

    def attend(c, operands):
        rows = chunk_rows(c)
        qt, qm, km, kd, decay = operands
        scores = [jnp.where(causal, _dot_nt(qm[:, ln], km[:, ln]), 0.0).astype(BF16) for ln in head_lanes]
        v_t = [_dot_nt(eye, v_ref[rows, ln]).astype(BF16) for ln in head_lanes]
        o_state = [_dot_nt(qt[:, ln], st_ref[hh].astype(BF16)) for hh, ln in enumerate(head_lanes)]
        o_intra = [_dot(scores[hh], v_ref[rows, ln]) for hh, ln in enumerate(head_lanes)]
        incr = [_dot(v_t[hh], kd[:, ln]) for hh, ln in enumerate(head_lanes)]
        for hh, ln in enumerate(head_lanes):
            ostate_ref[:, ln] = o_state[hh]
            st_ref[hh] = st_ref[hh] * decay[:, ln] + incr[hh]
            opre_ref[c % 2, :, ln] = o_intra[hh] + o_state[hh]

    def attend_exact(c):
        rows = chunk_rows(c)
        k, b = gates(c)
        q = q_ref[rows, :]
        v = v_ref[rows, :].astype(F32)
        t_idx = lax.broadcasted_iota(jnp.int32, (CHUNK, 1), 0)

        def source_row(s_idx, acc):
            pick = lambda a: jnp.sum(jnp.where(t_idx == s_idx, a, 0.0), axis=0, keepdims=True)
            b_s, k_s, v_s = pick(b), pick(k), pick(v)
            w = jnp.where(t_idx >= s_idx, q * jnp.exp(jnp.minimum(b - b_s, 0.0)) * k_s, 0.0)
            return acc + jnp.concatenate(
                [jnp.sum(w[:, ln], axis=-1, keepdims=True) * v_s[:, ln] for ln in head_lanes], axis=1)

        opre_ref[c % 2] = lax.fori_loop(0, CHUNK, source_row, ostate_ref[...])

    def step(c, behind):
        operands, largest_exponent = prepare(c)
        if behind:
            normalize_and_gate(c - 1)
        attend(c, operands)

        @pl.when(largest_exponent > MAX_EXPONENT)
        def _():
            attend_exact(c)

    def body(c, carry):
        step(c, True)
        return carry

    step(0, False)
    lax.fori_loop(1, n_chunks, body, 0)
    normalize_and_gate(n_chunks - 1)

    o_ref[...] = _zero_pad_rows(x + _dot(y_ref[...], wout_ref[...]), l * tl, l_real)


def _odd_mixer(h, g, w_in, lb, gn_g, w_out, *, l_real):
    b, lp, d = h.shape
    heads = d // HGRN_HEAD_DIM
    tile = pl.BlockSpec((None, SEQ_TILE, d), lambda i, j: (i, j, 0))
    params = (g, w_in, lb, gn_g, w_out)
    big = lambda dt: pltpu.VMEM((SEQ_TILE, d), dt)
    return pl.pallas_call(
        functools.partial(_odd_kernel, l_real=l_real),
        grid=(b, lp // SEQ_TILE),
        in_specs=[tile] + [_param_spec(p) for p in params],
        out_specs=tile,
        out_shape=jax.ShapeDtypeStruct(h.shape, F32),
        scratch_shapes=[big(F32)] * 3 + [big(BF16)] * 3 + [
            pltpu.VMEM((heads, HGRN_HEAD_DIM, HGRN_HEAD_DIM), F32),
            pltpu.VMEM((CHUNK, d), F32),
            pltpu.VMEM((2, CHUNK, d), F32)],
        compiler_params=pltpu.CompilerParams(
            dimension_semantics=("arbitrary", "arbitrary"), vmem_limit_bytes=VMEM_LIMIT),
        name="odd_mixer",
    )(h, *map(_operand, params))


def kernel(x, meta_tokens, mix_norm_g, mlp_norm_g, final_norm_g, ev_w_in, ev_conv_w, ev_conv_b, ev_ln_g,
           ev_ln_b, ev_pool_w, ev_pool_b, ev_pool_scale, ev_w_out, od_w_in, od_gnorm_g, od_w_out, lb_param,
           mlp_w1, mlp_w2):
    bn, seq, d = x.shape
    depth = mix_norm_g.shape[0]
    l_real = N_META + seq
    l_pad = -(-l_real // SEQ_TILE) * SEQ_TILE

    lb_all = jnp.cumsum(jax.nn.softmax(lb_param.astype(F32), axis=0), axis=0)
    lb_all = lb_all - lb_all[0]

    row = lambda v: v.reshape(1, -1).astype(F32)
    ev_w_in, ev_pool_w, ev_w_out, od_w_in, od_w_out = (
        w.astype(BF16) for w in (ev_w_in, ev_pool_w, ev_w_out, od_w_in, od_w_out))
    mlp_bf16 = {}
    h = x.astype(F32)
    for layer in range(depth):
        j = layer // 2
        if layer % 2 == 0:
            cast_layers = [m for m in (layer, layer + 1) if m < depth]
            h, cast = _even_mixer(h, row(mix_norm_g[layer]), _layer_of(ev_w_in, j), _layer_of(ev_conv_w, j),
                                  row(ev_conv_b[j]), row(ev_ln_g[j]), row(ev_ln_b[j]), _layer_of(ev_pool_w, j),
                                  row(ev_pool_b[j]), row(ev_pool_scale[j]), _layer_of(ev_w_out, j),
                                  l_real=l_real, l_pad=l_pad, meta=meta_tokens.astype(F32) if layer == 0 else None,
                                  to_bf16=[(w, m) for m in cast_layers for w in (mlp_w1, mlp_w2)])
            for n_done, m in enumerate(cast_layers):
                mlp_bf16[m] = cast[2 * n_done], cast[2 * n_done + 1]
        else:
            h = _odd_mixer(h, row(mix_norm_g[layer]), _layer_of(od_w_in, j), row(lb_all[layer]),
                           row(od_gnorm_g[j]), _layer_of(od_w_out, j), l_real=l_real)
        w1, w2 = mlp_bf16[layer]
        if layer == depth - 1:
            return _mlp_final(h, row(mlp_norm_g[layer]), row(final_norm_g), w1, w2, seq=seq)
        h = _mlp(h.reshape(bn * l_pad, d), row(mlp_norm_g[layer]), w1, w2).reshape(bn, l_pad, d)
```
